```python
import jax, jax.numpy as jnp
from jax import lax
import numpy as np

D_MODEL = 1024
BATCH = 8
SEQ = 4096
DEPTH = 1

LRU_WIDTH = 512
LRU_BLOCKS = 8
LRU_BLOCK_DIM = LRU_WIDTH // LRU_BLOCKS
LRU_CONV = 4
LRU_C = 8.0
N_HEADS = 8
HEAD_DIM = 64
ATT_WIDTH = N_HEADS * HEAD_DIM
MIX_WIDTH = LRU_WIDTH + ATT_WIDTH
IN_WIDTH = 2 * LRU_WIDTH + 3 * ATT_WIDTH
MOBA_BLOCK = 256
MOBA_TOPK = 3
Q_CHUNK = 32
D_FF = 3072
FFN_CONV = 3
PLE_DIM = 256
EPS = 1e-6
NEG = -1e30

kernel_name = "hymba_rglru_moba_convffn_ple"


def rmsnorm(x, g):
    xf = x.astype(jnp.float32)
    y = xf * lax.rsqrt(jnp.mean(xf * xf, axis=-1, keepdims=True) + EPS)
    return (y * g.astype(jnp.float32)).astype(x.dtype)


def causal_dwconv(x, w, b):
    K, C = w.shape
    y = lax.conv_general_dilated(
        x, w.reshape(K, 1, C).astype(x.dtype), window_strides=(1,), padding=[(K - 1, 0)],
        dimension_numbers=("NWC", "WIO", "NWC"), feature_group_count=C)
    return y + b.astype(x.dtype)


def rglru(xc, wa, ba, wx, bx, lam):
    B, T, W = xc.shape
    xb = xc.reshape(B, T, LRU_BLOCKS, LRU_BLOCK_DIM)
    r = jax.nn.sigmoid(jnp.einsum("btnc,ncd->btnd", xb, wa).reshape(B, T, W) + ba)
    i = jax.nn.sigmoid(jnp.einsum("btnc,ncd->btnd", xb, wx).reshape(B, T, W) + bx)
    log_a = -LRU_C * r.astype(jnp.float32) * jax.nn.softplus(-lam.astype(jnp.float32))
    a = jnp.exp(log_a)
    mult = jnp.sqrt(-jnp.expm1(2.0 * log_a))
    mult = mult.at[:, 0].set(1.0)
    b = mult * (i * xc).astype(jnp.float32)

    def combine(left, right):
        a1, b1 = left
        a2, b2 = right
        return a1 * a2, a2 * b1 + b2

    _, hs = lax.associative_scan(combine, (a, b), axis=1)
    return hs.astype(xc.dtype)


def moba_attention(q, k, v, slopes):
    B, T, H, Dh = q.shape
    L = MOBA_BLOCK
    NB = -(-T // L)
    Tp = NB * L
    if Tp != T:
        pad = ((0, 0), (0, Tp - T), (0, 0), (0, 0))
        q, k, v = jnp.pad(q, pad), jnp.pad(k, pad), jnp.pad(v, pad)
    scale = 1.0 / np.sqrt(Dh)
    qh = q.transpose(0, 2, 1, 3)
    kb = k.transpose(0, 2, 1, 3).reshape(B, H, NB, L, Dh)
    vb = v.transpose(0, 2, 1, 3).reshape(B, H, NB, L, Dh)

    k_mean = jnp.mean(kb.astype(jnp.float32), axis=3)
    gate = jnp.einsum("bhtd,bhnd->bhtn", qh.astype(jnp.float32), k_mean)
    q_blk = jnp.arange(Tp) // L
    past = jnp.arange(NB)[None, :] < q_blk[:, None]
    gate = jnp.where(past[None, None], gate, NEG)
    n_sel = min(MOBA_TOPK, NB)
    _, sel = lax.top_k(gate, n_sel)
    valid = jnp.arange(n_sel)[None, :] < jnp.minimum(q_blk, n_sel)[:, None]

    n_chunks = Tp // Q_CHUNK
    q_c = qh.reshape(B, H, n_chunks, Q_CHUNK, Dh).transpose(2, 0, 1, 3, 4)
    sel_c = sel.reshape(B, H, n_chunks, Q_CHUNK, n_sel).transpose(2, 0, 1, 3, 4)
    valid_c = valid.reshape(n_chunks, Q_CHUNK, n_sel)
    idx_c = jnp.arange(n_chunks, dtype=jnp.int32)
    bi = jnp.arange(B)[:, None, None, None]
    hi = jnp.arange(H)[None, :, None, None]
    slope_f = slopes.astype(jnp.float32)

    def body(args):
        qc, selc, validc, c = args
        t_pos = c * Q_CHUNK + jnp.arange(Q_CHUNK)
        qb = (c * Q_CHUNK) // L
        ks = kb[bi, hi, selc]
        vs = vb[bi, hi, selc]
        s_pos = selc[..., None] * L + jnp.arange(L)
        dist_sel = (t_pos[None, None, :, None, None] - s_pos).astype(jnp.float32)
        lg_sel = jnp.einsum("bhcd,bhcsld->bhcsl", qc, ks, preferred_element_type=jnp.float32) * scale
        lg_sel = lg_sel - slope_f[None, :, None, None, None] * dist_sel
        lg_sel = jnp.where(validc[None, None, :, :, None], lg_sel, NEG)
        k_own = lax.dynamic_index_in_dim(kb, qb, axis=2, keepdims=False)
        v_own = lax.dynamic_index_in_dim(vb, qb, axis=2, keepdims=False)
        s_own = qb * L + jnp.arange(L)
        dist_own = (t_pos[:, None] - s_own[None, :]).astype(jnp.float32)
        lg_own = jnp.einsum("bhcd,bhld->bhcl", qc, k_own, preferred_element_type=jnp.float32) * scale
        lg_own = lg_own - slope_f[None, :, None, None] * dist_own[None, None]
        lg_own = jnp.where((dist_own >= 0)[None, None], lg_own, NEG)
        logits = jnp.concatenate([lg_sel.reshape(B, H, Q_CHUNK, n_sel * L), lg_own], axis=-1)
        w = jax.nn.softmax(logits, axis=-1).astype(vb.dtype)
        w_sel = w[..., : n_sel * L].reshape(B, H, Q_CHUNK, n_sel, L)
        w_own = w[..., n_sel * L:]
        out = (jnp.einsum("bhcsl,bhcsld->bhcd", w_sel, vs, preferred_element_type=jnp.float32)
               + jnp.einsum("bhcl,bhld->bhcd", w_own, v_own, preferred_element_type=jnp.float32))
        return out.astype(vb.dtype)

    out = lax.map(body, (q_c, sel_c, valid_c, idx_c))
    out = out.transpose(1, 0, 3, 2, 4).reshape(B, Tp, H * Dh)
    return out[:, :T]


def setup_inputs(seed: int = 0) -> dict:
    key = jax.random.key(seed)
    ks = jax.random.split(key, 24)
    f32 = jnp.float32

    def nrm(k, shape, scale):
        return jax.random.normal(k, shape, f32) * scale

    def gain(k, shape):
        return 1.0 + 0.05 * jax.random.normal(k, shape, f32)

    u = jax.random.uniform(ks[9], (DEPTH, LRU_WIDTH), f32, 0.9, 0.999)
    a0 = u ** (1.0 / LRU_C)
    lru_lambda = jnp.log(a0) - jnp.log1p(-a0)
    return {
        "x": nrm(ks[0], (BATCH, SEQ, D_MODEL), 1.0),
        "p": nrm(ks[1], (DEPTH, BATCH, SEQ, PLE_DIM), 1.0),
        "ln_mix": gain(ks[2], (DEPTH, D_MODEL)),
        "w_in": nrm(ks[3], (DEPTH, D_MODEL, IN_WIDTH), D_MODEL ** -0.5),
        "conv_w": nrm(ks[4], (DEPTH, LRU_CONV, LRU_WIDTH), LRU_CONV ** -0.5),
        "conv_b": nrm(ks[5], (DEPTH, LRU_WIDTH), 0.01),
        "gate_a_w": nrm(ks[6], (DEPTH, LRU_BLOCKS, LRU_BLOCK_DIM, LRU_BLOCK_DIM), LRU_BLOCK_DIM ** -0.5),
        "gate_a_b": nrm(ks[7], (DEPTH, LRU_WIDTH), 0.01),
        "gate_x_w": nrm(ks[8], (DEPTH, LRU_BLOCKS, LRU_BLOCK_DIM, LRU_BLOCK_DIM), LRU_BLOCK_DIM ** -0.5),
        "gate_x_b": nrm(ks[10], (DEPTH, LRU_WIDTH), 0.01),
        "lru_lambda": lru_lambda,
        "q_gain": gain(ks[11], (DEPTH, HEAD_DIM)),
        "k_gain": gain(ks[12], (DEPTH, HEAD_DIM)),
        "w_out": nrm(ks[13], (DEPTH, MIX_WIDTH, D_MODEL), MIX_WIDTH ** -0.5),
        "ln_ffn": gain(ks[14], (DEPTH, D_MODEL)),
        "w_up": nrm(ks[15], (DEPTH, D_MODEL, 2 * D_FF), D_MODEL ** -0.5),
        "ffn_conv_w": nrm(ks[16], (DEPTH, FFN_CONV, D_FF), FFN_CONV ** -0.5),
        "ffn_conv_b": nrm(ks[17], (DEPTH, D_FF), 0.01),
        "w_down": nrm(ks[18], (DEPTH, D_FF, D_MODEL), D_FF ** -0.5),
        "ln_ple": gain(ks[19], (DEPTH, D_MODEL)),
        "w_ple_gate": nrm(ks[20], (DEPTH, D_MODEL, D_MODEL), D_MODEL ** -0.5),
        "w_ple_proj": nrm(ks[21], (DEPTH, PLE_DIM, D_MODEL), PLE_DIM ** -0.5),
    }


def reference(x, p, ln_mix, w_in, conv_w, conv_b, gate_a_w, gate_a_b, gate_x_w, gate_x_b,
              lru_lambda, q_gain, k_gain, w_out, ln_ffn, w_up, ffn_conv_w, ffn_conv_b, w_down,
              ln_ple, w_ple_gate, w_ple_proj):
    B, T, _ = x.shape
    slopes = 2.0 ** (-8.0 * jnp.arange(1, N_HEADS + 1, dtype=jnp.float32) / N_HEADS)
    split_at = [LRU_WIDTH, 2 * LRU_WIDTH, 2 * LRU_WIDTH + ATT_WIDTH, 2 * LRU_WIDTH + 2 * ATT_WIDTH]
    h = x
    for i in range(DEPTH):
        u = rmsnorm(h, ln_mix[i])
        proj = u @ w_in[i]
        x_lru, g_lru, q, k, v = jnp.split(proj, split_at, axis=-1)
        xc = causal_dwconv(x_lru, conv_w[i], conv_b[i])
        y_lru = rglru(xc, gate_a_w[i], gate_a_b[i], gate_x_w[i], gate_x_b[i], lru_lambda[i])
        y_lru = y_lru * jax.nn.gelu(g_lru)
        q = rmsnorm(q.reshape(B, T, N_HEADS, HEAD_DIM), q_gain[i])
        k = rmsnorm(k.reshape(B, T, N_HEADS, HEAD_DIM), k_gain[i])
        v = v.reshape(B, T, N_HEADS, HEAD_DIM)
        y_att = moba_attention(q, k, v, slopes)
        h = h + jnp.concatenate([y_lru, y_att], axis=-1) @ w_out[i]
        u = rmsnorm(h, ln_ffn[i])
        g_up, v_up = jnp.split(u @ w_up[i], 2, axis=-1)
        g_up = causal_dwconv(g_up, ffn_conv_w[i], ffn_conv_b[i])
        h = h + (jax.nn.gelu(g_up) * v_up) @ w_down[i]
        gate = jax.nn.sigmoid(rmsnorm(h, ln_ple[i]) @ w_ple_gate[i])
        h = h + gate * (p[i].astype(h.dtype) @ w_ple_proj[i])
    return h
```

```python
import functools

import numpy as np
import jax
import jax.numpy as jnp
from jax import lax
from jax.experimental import pallas as pl
from jax.experimental.pallas import tpu as pltpu

F32 = jnp.float32
BF16 = jnp.bfloat16

LRU_WIDTH = 512
LRU_BLOCKS = 8
LRU_BLOCK_DIM = LRU_WIDTH // LRU_BLOCKS
LRU_CONV = 4
LRU_C = 8.0
N_HEADS = 8
HEAD_DIM = 64
ATT_WIDTH = N_HEADS * HEAD_DIM
MOBA_BLOCK = 256
MOBA_TOPK = 3
FFN_CONV = 3
EPS = 1e-6
NEG = -1e30

V7X_LANES = 128
V7X_SUBLANES = 8
V7X_MXU_DIM = 256
V7X_VMEM_BYTES = 64 * 1024 * 1024

ROW_TILE = 512
HEAD_PAIR = V7X_LANES // HEAD_DIM
FF_CHUNK = 1024
PAST_BUCKETS = (0, 4, 8, 12, 15)
N_ALIBI_ROWS = V7X_SUBLANES
VMEM_LIMIT = 56 * 1024 * 1024


def _const_spec(shape):
    nd = len(shape)
    return pl.BlockSpec(shape, lambda *_: (0,) * nd, pipeline_mode=pl.Buffered(1))


def _rmsnorm(x, g):
    return x * lax.rsqrt(jnp.mean(x * x, axis=-1, keepdims=True) + EPS) * g


def _dot(a, b):
    return jnp.dot(a, b, preferred_element_type=F32)


def _segment_sumsq(z, seg_ref):
    sq = z * z
    parts = []
    for hf in range(z.shape[1] // V7X_MXU_DIM):
        s = sq[:, hf * V7X_MXU_DIM:(hf + 1) * V7X_MXU_DIM]
        hi = s.astype(BF16)
        lo = (s - hi.astype(F32)).astype(BF16)
        parts.append(_dot(hi, seg_ref[...]) + _dot(lo, seg_ref[...]))
    return jnp.concatenate(parts, axis=1)


def _linear_scan(a, b):
    n = a.shape[0]
    row = lax.broadcasted_iota(jnp.int32, a.shape, 0)
    s = 1
    while s < n:
        keep = row >= s
        a_sh = jnp.where(keep, pltpu.roll(a, s, 0), 1.0)
        b_sh = jnp.where(keep, pltpu.roll(b, s, 0), 0.0)
        b = b + a * b_sh
        a = a * a_sh
        s *= 2
    return a, b


def _inproj_kernel(x_ref, ln_ref, win_ref, cw_ref, cb_ref, wg_ref, ba_ref, bx_ref, lam_ref,
                   qg_ref, kg_ref, seg_ref,
                   ylru_ref, q_ref, k_ref, v_ref,
                   xbuf, hcarry):
    t = pl.program_id(1)
    tt = x_ref.shape[1]

    @pl.when(t == 0)
    def _():
        xbuf[0:V7X_SUBLANES, :] = jnp.zeros((V7X_SUBLANES, LRU_WIDTH), F32)
        hcarry[...] = jnp.zeros_like(hcarry)

    u = _rmsnorm(x_ref[0], ln_ref[...])
    proj = _dot(u.astype(BF16), win_ref[...])
    x_lru = proj[:, 0:LRU_WIDTH]
    g_lru = proj[:, LRU_WIDTH:2 * LRU_WIDTH]
    o = 2 * LRU_WIDTH
    q = proj[:, o:o + ATT_WIDTH]
    k = proj[:, o + ATT_WIDTH:o + 2 * ATT_WIDTH]
    v = proj[:, o + 2 * ATT_WIDTH:o + 3 * ATT_WIDTH]

    xbuf[V7X_SUBLANES:V7X_SUBLANES + tt, :] = x_lru
    xc = cb_ref[...] + cw_ref[LRU_CONV - 1:LRU_CONV, :] * x_lru
    for kk in range(LRU_CONV - 1):
        start = V7X_SUBLANES - (LRU_CONV - 1) + kk
        xc = xc + cw_ref[kk:kk + 1, :] * xbuf[pl.ds(start, tt), :]
    xbuf[0:V7X_SUBLANES, :] = x_lru[tt - V7X_SUBLANES:tt, :]

    xcb = xc.astype(BF16)
    r_parts, i_parts = [], []
    for hf in range(LRU_WIDTH // V7X_MXU_DIM):
        z = _dot(xcb[:, hf * V7X_MXU_DIM:(hf + 1) * V7X_MXU_DIM], wg_ref[hf])
        r_parts.append(z[:, 0:V7X_MXU_DIM])
        i_parts.append(z[:, V7X_MXU_DIM:2 * V7X_MXU_DIM])
    r = jax.nn.sigmoid(jnp.concatenate(r_parts, axis=1) + ba_ref[...])
    ig = jax.nn.sigmoid(jnp.concatenate(i_parts, axis=1) + bx_ref[...])

    nl = -lam_ref[...]
    softplus = jnp.maximum(nl, 0.0) + jnp.log(1.0 + jnp.exp(-jnp.abs(nl)))
    log_a = (-LRU_C) * r * softplus
    a = jnp.exp(log_a)
    mult = jnp.sqrt(1.0 - a * a)
    row = lax.broadcasted_iota(jnp.int32, a.shape, 0) + t * tt
    mult = jnp.where(row == 0, 1.0, mult)
    b = mult * (ig * xc)

    a_cum, h0 = _linear_scan(a, b)
    h = h0 + a_cum * hcarry[...]
    hcarry[...] = h[tt - 1:tt, :]
    ylru_ref[0] = (h * jax.nn.gelu(g_lru)).astype(ylru_ref.dtype)

    qn = q * lax.rsqrt(_segment_sumsq(q, seg_ref) * (1.0 / HEAD_DIM) + EPS) * qg_ref[...]
    kn = k * lax.rsqrt(_segment_sumsq(k, seg_ref) * (1.0 / HEAD_DIM) + EPS) * kg_ref[...]
    q_ref[0] = qn.astype(q_ref.dtype)
    k_ref[0] = kn.astype(k_ref.dtype)
    v_ref[0] = v.astype(v_ref.dtype)


def _inproj_call(x, ln, win, cw, cb, wg, ba, bx, lam, qg, kg, seg):
    B, T, D = x.shape
    tt = min(ROW_TILE, T)
    row_spec = lambda w: pl.BlockSpec((1, tt, w), lambda b, t: (b, t, 0))
    consts = (ln, win, cw, cb, wg, ba, bx, lam, qg, kg, seg)
    return pl.pallas_call(
        _inproj_kernel,
        grid=(B, T // tt),
        in_specs=[row_spec(D)] + [_const_spec(c.shape) for c in consts],
        out_specs=[row_spec(LRU_WIDTH), row_spec(ATT_WIDTH), row_spec(ATT_WIDTH), row_spec(ATT_WIDTH)],
        out_shape=[jax.ShapeDtypeStruct((B, T, LRU_WIDTH), BF16),
                   jax.ShapeDtypeStruct((B, T, ATT_WIDTH), F32),
                   jax.ShapeDtypeStruct((B, T, ATT_WIDTH), F32),
                   jax.ShapeDtypeStruct((B, T, ATT_WIDTH), BF16)],
        scratch_shapes=[pltpu.VMEM((tt + V7X_SUBLANES, LRU_WIDTH), F32),
                        pltpu.VMEM((1, LRU_WIDTH), F32)],
        compiler_params=pltpu.CompilerParams(
            dimension_semantics=("arbitrary", "arbitrary"), vmem_limit_bytes=VMEM_LIMIT),
        name="inproj_rglru",
    )(x, *consts)


def _select_bias(gate, i, nb):
    rowi = lax.broadcasted_iota(jnp.int32, gate.shape, 0)
    rowf = rowi.astype(F32)
    ninf = -jnp.inf
    g = jnp.where(rowi < i, gate, ninf)
    sel = jnp.zeros(gate.shape, F32)
    for _ in range(min(MOBA_TOPK, nb)):
        mx = jnp.max(g, axis=0, keepdims=True)
        cand = jnp.logical_and(g == mx, g > ninf)
        first = jnp.min(jnp.where(cand, rowf, float(nb)), axis=0, keepdims=True)
        pick = jnp.logical_and(rowf == first, cand)
        sel = jnp.where(pick, 1.0, sel)
        g = jnp.where(pick, ninf, g)
    return jnp.where(sel > 0.0, 0.0, NEG)


def _attn_kernel(q_ref, k_ref, v_ref, kx_ref, slope_ref, o_ref,
                 kp_ref, vt_ref, km_ref, s_ref):
    i = pl.program_id(2)
    L = MOBA_BLOCK
    nb = k_ref.shape[1] // L
    W = HEAD_PAIR * HEAD_DIM

    @pl.when(i == 0)
    def _():
        for j in range(nb):
            kj = k_ref[0, j * L:(j + 1) * L, :]
            kp_ref[j, :, 0:W] = kj.astype(BF16)
            kp_ref[j, :, W:2 * W] = kx_ref[j * L:(j + 1) * L, :]
            km_ref[j:j + 1, :] = jnp.mean(kj, axis=0, keepdims=True)
            vt_ref[j] = v_ref[0, j * L:(j + 1) * L, :].astype(F32).T.astype(BF16)

    qf = q_ref[0]
    lane = lax.broadcasted_iota(jnp.int32, qf.shape, 1)
    krow = lax.broadcasted_iota(jnp.int32, (L, L), 0)
    qcol = lax.broadcasted_iota(jnp.int32, (L, L), 1)
    causal = krow <= qcol
    pad_rows = jnp.zeros((W - nb - N_ALIBI_ROWS, L), F32)
    scale = 1.0 / np.sqrt(HEAD_DIM)

    def head_inputs(hh):
        qm = jnp.where(jnp.logical_and(lane >= hh * HEAD_DIM, lane < (hh + 1) * HEAD_DIM), qf, 0.0)
        gate = lax.dot_general(km_ref[...], qm, (((1,), (1,)), ((), ())),
                               precision=lax.Precision.HIGHEST, preferred_element_type=F32)
        bias = _select_bias(gate, i, nb)
        qt = (qm * scale).T
        slope_rows = slope_ref[hh]
        qp_past = jnp.concatenate([qt, bias, slope_rows, pad_rows], axis=0).astype(BF16)
        qp_diag = jnp.concatenate([qt, jnp.zeros_like(bias), slope_rows, pad_rows], axis=0).astype(BF16)
        return qp_past, qp_diag

    def attend(n_past):
        accs, ls = [], []
        for hh in range(HEAD_PAIR):
            qp_past, qp_diag = head_inputs(hh)
            sd = jnp.where(causal, _dot(kp_ref[i], qp_diag), NEG)
            m8 = jnp.max(sd.reshape(L // V7X_SUBLANES, V7X_SUBLANES, L), axis=0)
            for c in range(n_past):
                s = _dot(kp_ref[c], qp_past)
                s_ref[c] = s
                m8 = jnp.maximum(m8, jnp.max(s.reshape(L // V7X_SUBLANES, V7X_SUBLANES, L), axis=0))
            m = jnp.max(m8, axis=0, keepdims=True)
            p = jnp.exp(sd - m)
            l8 = jnp.sum(p.reshape(L // V7X_SUBLANES, V7X_SUBLANES, L), axis=0)
            acc = _dot(vt_ref[i], p.astype(BF16))
            for c in range(n_past):
                p = jnp.exp(s_ref[c] - m)
                l8 = l8 + jnp.sum(p.reshape(L // V7X_SUBLANES, V7X_SUBLANES, L), axis=0)
                acc = acc + _dot(vt_ref[c], p.astype(BF16))
            accs.append(acc)
            ls.append(jnp.sum(l8, axis=0, keepdims=True))
        orow = lax.broadcasted_iota(jnp.int32, (W, L), 0)
        out_t = accs[HEAD_PAIR - 1] / ls[HEAD_PAIR - 1]
        for hh in range(HEAD_PAIR - 2, -1, -1):
            out_t = jnp.where(orow < (hh + 1) * HEAD_DIM, accs[hh] / ls[hh], out_t)
        o_ref[0] = out_t.T.astype(o_ref.dtype)

    buckets = [n for n in PAST_BUCKETS if n < nb]
    if buckets[-1] != nb - 1:
        buckets.append(nb - 1)
    lo = -1
    for n_past in buckets:
        pl.when(jnp.logical_and(i > lo, i <= n_past))(functools.partial(attend, n_past))
        lo = n_past


def _attn_call(q, k, v, kx, slope_rows):
    B, T, _ = q.shape
    L = MOBA_BLOCK
    nb = T // L
    W = HEAD_PAIR * HEAD_DIM
    npairs = N_HEADS // HEAD_PAIR
    return pl.pallas_call(
        _attn_kernel,
        grid=(B, npairs, nb),
        in_specs=[pl.BlockSpec((1, L, W), lambda b, g, i: (b, i, g)),
                  pl.BlockSpec((1, T, W), lambda b, g, i: (b, 0, g)),
                  pl.BlockSpec((1, T, W), lambda b, g, i: (b, 0, g)),
                  _const_spec(kx.shape),
                  pl.BlockSpec((HEAD_PAIR, N_ALIBI_ROWS, L), lambda b, g, i: (g, 0, 0))],
        out_specs=pl.BlockSpec((1, L, W), lambda b, g, i: (b, i, g)),
        out_shape=jax.ShapeDtypeStruct((B, T, ATT_WIDTH), BF16),
        scratch_shapes=[pltpu.VMEM((nb, L, 2 * W), BF16),
                        pltpu.VMEM((nb, W, L), BF16),
                        pltpu.VMEM((nb, W), F32),
                        pltpu.VMEM((max(nb - 1, 1), L, L), F32)],
        compiler_params=pltpu.CompilerParams(
            dimension_semantics=("arbitrary", "arbitrary", "arbitrary"), vmem_limit_bytes=VMEM_LIMIT),
        name="moba_attention",
    )(q, k, v, kx, slope_rows)


def _ffn_kernel(x_ref, ylru_ref, yatt_ref, p_ref, wout_ref, lnf_ref, wup_ref, fw_ref, fb_ref, wdown_ref,
                lnp_ref, wg_ref, wp_ref, o_ref, gbuf, gcar):
    t = pl.program_id(1)
    tt = x_ref.shape[1]
    d_ff = wdown_ref.shape[0]
    pad = V7X_SUBLANES

    @pl.when(t == 0)
    def _():
        gcar[...] = jnp.zeros_like(gcar)

    h = (x_ref[0] + _dot(ylru_ref[0], wout_ref[0:LRU_WIDTH, :])
         + _dot(yatt_ref[0], wout_ref[LRU_WIDTH:LRU_WIDTH + ATT_WIDTH, :]))
    u = _rmsnorm(h, lnf_ref[...]).astype(BF16)
    acc = jnp.zeros_like(h)
    for c in range(d_ff // FF_CHUNK):
        lo = c * FF_CHUNK
        g = _dot(u, wup_ref[:, lo:lo + FF_CHUNK])
        vv = _dot(u, wup_ref[:, d_ff + lo:d_ff + lo + FF_CHUNK])
        gbuf[0:pad, :] = gcar[c]
        gbuf[pad:pad + tt, :] = g
        gcar[c] = g[tt - pad:tt, :]
        gc = fb_ref[:, lo:lo + FF_CHUNK] + fw_ref[FFN_CONV - 1:FFN_CONV, lo:lo + FF_CHUNK] * g
        for kk in range(FFN_CONV - 1):
            gc = gc + fw_ref[kk:kk + 1, lo:lo + FF_CHUNK] * gbuf[pl.ds(pad - (FFN_CONV - 1) + kk, tt), :]
        act = (jax.nn.gelu(gc) * vv).astype(BF16)
        acc = acc + _dot(act, wdown_ref[lo:lo + FF_CHUNK, :])
    h = h + acc
    gate = jax.nn.sigmoid(_dot(_rmsnorm(h, lnp_ref[...]).astype(BF16), wg_ref[...]))
    o_ref[0] = h + gate * _dot(p_ref[0].astype(BF16), wp_ref[...])


def _ffn_call(x, ylru, yatt, p, wout, lnf, wup, fw, fb, wdown, lnp, wg, wp):
    B, T, D = x.shape
    tt = min(ROW_TILE, T)
    d_ff = wdown.shape[0]
    row_spec = lambda w: pl.BlockSpec((1, tt, w), lambda b, t: (b, t, 0))
    consts = (wout, lnf, wup, fw, fb, wdown, lnp, wg, wp)
    return pl.pallas_call(
        _ffn_kernel,
        grid=(B, T // tt),
        in_specs=[row_spec(D), row_spec(LRU_WIDTH), row_spec(ATT_WIDTH), row_spec(p.shape[-1])]
                 + [_const_spec(c.shape) for c in consts],
        out_specs=row_spec(D),
        out_shape=jax.ShapeDtypeStruct((B, T, D), x.dtype),
        scratch_shapes=[pltpu.VMEM((tt + V7X_SUBLANES, FF_CHUNK), F32),
                        pltpu.VMEM((d_ff // FF_CHUNK, V7X_SUBLANES, FF_CHUNK), F32)],
        compiler_params=pltpu.CompilerParams(
            dimension_semantics=("arbitrary", "arbitrary"), vmem_limit_bytes=VMEM_LIMIT),
        name="outproj_ffn_ple",
    )(x, ylru, yatt, p, *consts)


def _block_diag(w):
    n, c, d = w.shape
    return jnp.einsum("ncd,nm->ncmd", w, jnp.eye(n, dtype=w.dtype)).reshape(n * c, n * d)


def _gate_weights(wa, wx):
    per = V7X_MXU_DIM // LRU_BLOCK_DIM
    groups = []
    for hf in range(LRU_WIDTH // V7X_MXU_DIM):
        sl = slice(hf * per, (hf + 1) * per)
        groups.append(jnp.concatenate([_block_diag(wa[sl]), _block_diag(wx[sl])], axis=1))
    return jnp.stack(groups).astype(BF16)


def _split3_bf16(s):
    s1 = s.astype(BF16).astype(F32)
    s2 = (s - s1).astype(BF16).astype(F32)
    s3 = (s - s1 - s2).astype(BF16).astype(F32)
    return s1, s2, s3


def _attention_constants(T):
    nb = T // MOBA_BLOCK
    W = HEAD_PAIR * HEAD_DIM
    pos = jnp.arange(T, dtype=jnp.int32)
    onehot = (pos[:, None] // MOBA_BLOCK == jnp.arange(nb)[None, :]).astype(F32)
    hi = ((pos >> 6) << 6).astype(F32)[:, None]
    lo = (pos & 63).astype(F32)[:, None]
    cols = [onehot, hi, hi, hi, lo, lo, lo]
    used = nb + 6
    kx = jnp.concatenate(cols + [jnp.zeros((T, W - used), F32)], axis=1).astype(BF16)
    slopes = 2.0 ** (-8.0 * jnp.arange(1, N_HEADS + 1, dtype=F32) / N_HEADS)
    s1, s2, s3 = _split3_bf16(slopes)
    rows = jnp.stack([s1, s2, s3, s1, s2, s3] + [jnp.zeros_like(s1)] * (N_ALIBI_ROWS - 6), axis=1)
    slope_rows = jnp.broadcast_to(rows[:, :, None], (N_HEADS, N_ALIBI_ROWS, MOBA_BLOCK)).astype(F32)
    return kx, slope_rows


def kernel(x, p, ln_mix, w_in, conv_w, conv_b, gate_a_w, gate_a_b, gate_x_w, gate_x_b, lru_lambda, q_gain,
           k_gain, w_out, ln_ffn, w_up, ffn_conv_w, ffn_conv_b, w_down, ln_ple, w_ple_gate, w_ple_proj):
    B, T, D = x.shape
    depth = w_in.shape[0]
    assert T % MOBA_BLOCK == 0 and T % min(ROW_TILE, T) == 0
    row = lambda a: a.reshape(1, -1)
    seg = _block_diag(jnp.ones((V7X_MXU_DIM // HEAD_DIM, HEAD_DIM, HEAD_DIM), F32)).astype(BF16)
    kx, slope_rows = _attention_constants(T)
    h = x
    for i in range(depth):
        ylru, q, k, v = _inproj_call(
            h, row(ln_mix[i]), w_in[i].astype(BF16), conv_w[i], row(conv_b[i]),
            _gate_weights(gate_a_w[i], gate_x_w[i]), row(gate_a_b[i]), row(gate_x_b[i]), row(lru_lambda[i]),
            row(jnp.tile(q_gain[i], N_HEADS)), row(jnp.tile(k_gain[i], N_HEADS)), seg)
        yatt = _attn_call(q, k, v, kx, slope_rows)
        h = _ffn_call(
            h, ylru, yatt, p[i].astype(h.dtype), w_out[i].astype(BF16), row(ln_ffn[i]), w_up[i].astype(BF16),
            ffn_conv_w[i], row(ffn_conv_b[i]), w_down[i].astype(BF16), row(ln_ple[i]),
            w_ple_gate[i].astype(BF16), w_ple_proj[i].astype(BF16))
    return h
```

```python
import functools

import numpy as np
import jax
import jax.numpy as jnp
from jax import lax
from jax.experimental import pallas as pl
from jax.experimental.pallas import tpu as pltpu

F32 = jnp.float32
BF16 = jnp.bfloat16

LRU_WIDTH = 512
LRU_BLOCKS = 8
LRU_BLOCK_DIM = LRU_WIDTH // LRU_BLOCKS
LRU_CONV = 4
LRU_C = 8.0
N_HEADS = 8
HEAD_DIM = 64
ATT_WIDTH = N_HEADS * HEAD_DIM
MOBA_BLOCK = 256
MOBA_TOPK = 3
FFN_CONV = 3
EPS = 1e-6
NEG = -1e30
LOG2E = float(np.log2(np.e))

V7X_LANES = 128
V7X_SUBLANES = 8
V7X_MXU_DIM = 256
V7X_VMEM_BYTES = 64 * 1024 * 1024

ROW_TILE = 512
HEAD_PAIR = V7X_LANES // HEAD_DIM
FF_CHUNK = 1024
N_ALIBI_ROWS = V7X_SUBLANES
ATTN_LOOKAHEAD = 2
VMEM_LIMIT = 56 * 1024 * 1024


def _const_spec(shape):
    nd = len(shape)
    return pl.BlockSpec(shape, lambda *_: (0,) * nd, pipeline_mode=pl.Buffered(1))


def _rmsnorm(x, g):
    return x * lax.rsqrt(jnp.mean(x * x, axis=-1, keepdims=True) + EPS) * g


def _dot(a, b):
    return jnp.dot(a, b, preferred_element_type=F32)


def _segment_sumsq(z, seg_ref):
    sq = z * z
    parts = []
    for hf in range(z.shape[1] // V7X_MXU_DIM):
        s = sq[:, hf * V7X_MXU_DIM:(hf + 1) * V7X_MXU_DIM]
        hi = s.astype(BF16)
        lo = (s - hi.astype(F32)).astype(BF16)
        parts.append(_dot(hi, seg_ref[...]) + _dot(lo, seg_ref[...]))
    return jnp.concatenate(parts, axis=1)


def _linear_scan(a, b, h_in):
    n, c = a.shape
    sl = V7X_SUBLANES
    a3 = a.reshape(n // sl, sl, c)
    b3 = b.reshape(n // sl, sl, c)
    sub = lax.broadcasted_iota(jnp.int32, a3.shape, 1)
    s = 1
    while s < sl:
        keep = sub >= s
        a_sh = jnp.where(keep, pltpu.roll(a3, s, 1), 1.0)
        b_sh = jnp.where(keep, pltpu.roll(b3, s, 1), 0.0)
        b3 = b3 + a3 * b_sh
        a3 = a3 * a_sh
        s *= 2
    carry = h_in
    groups = []
    for g in range(n // sl):
        hg = b3[g] + a3[g] * carry
        groups.append(hg)
        carry = hg[sl - 1:sl, :]
    return jnp.concatenate(groups, axis=0)


def _inproj_kernel(x_ref, ln_ref, win_ref, cw_ref, cb_ref, wg_ref, ba_ref, bx_ref, lam_ref,
                   qg_ref, kg_ref, seg_ref,
                   ylru_ref, q_ref, k_ref, v_ref,
                   xbuf, hcarry):
    t = pl.program_id(1)
    tt = x_ref.shape[1]

    @pl.when(t == 0)
    def _():
        xbuf[0:V7X_SUBLANES, :] = jnp.zeros((V7X_SUBLANES, LRU_WIDTH), F32)
        hcarry[...] = jnp.zeros_like(hcarry)

    u = _rmsnorm(x_ref[0], ln_ref[...])
    proj = _dot(u.astype(BF16), win_ref[...])
    x_lru = proj[:, 0:LRU_WIDTH]
    g_lru = proj[:, LRU_WIDTH:2 * LRU_WIDTH]
    o = 2 * LRU_WIDTH
    q = proj[:, o:o + ATT_WIDTH]
    k = proj[:, o + ATT_WIDTH:o + 2 * ATT_WIDTH]
    v = proj[:, o + 2 * ATT_WIDTH:o + 3 * ATT_WIDTH]

    xbuf[V7X_SUBLANES:V7X_SUBLANES + tt, :] = x_lru
    xc = cb_ref[...] + cw_ref[LRU_CONV - 1:LRU_CONV, :] * x_lru
    for kk in range(LRU_CONV - 1):
        start = V7X_SUBLANES - (LRU_CONV - 1) + kk
        xc = xc + cw_ref[kk:kk + 1, :] * xbuf[pl.ds(start, tt), :]
    xbuf[0:V7X_SUBLANES, :] = x_lru[tt - V7X_SUBLANES:tt, :]

    xcb = xc.astype(BF16)
    r_parts, i_parts = [], []
    for hf in range(LRU_WIDTH // V7X_MXU_DIM):
        z = _dot(xcb[:, hf * V7X_MXU_DIM:(hf + 1) * V7X_MXU_DIM], wg_ref[hf])
        r_parts.append(z[:, 0:V7X_MXU_DIM])
        i_parts.append(z[:, V7X_MXU_DIM:2 * V7X_MXU_DIM])
    r = jax.nn.sigmoid(jnp.concatenate(r_parts, axis=1) + ba_ref[...])
    ig = jax.nn.sigmoid(jnp.concatenate(i_parts, axis=1) + bx_ref[...])

    nl = -lam_ref[...]
    softplus = jnp.maximum(nl, 0.0) + jnp.log(1.0 + jnp.exp(-jnp.abs(nl)))
    log_a = (-LRU_C) * r * softplus
    a = jnp.exp(log_a)
    z = 1.0 - a * a
    mult = jnp.where(z > 0.0, z * lax.rsqrt(z), 0.0)
    gx = ig * xc
    b = mult * gx
    first = lax.broadcasted_iota(jnp.int32, (V7X_SUBLANES, LRU_WIDTH), 0) + t * tt == 0
    b = jnp.concatenate([jnp.where(first, gx[0:V7X_SUBLANES, :], b[0:V7X_SUBLANES, :]), b[V7X_SUBLANES:, :]], axis=0)

    h = _linear_scan(a, b, hcarry[...])
    hcarry[...] = h[tt - 1:tt, :]
    ylru_ref[0] = (h * jax.nn.gelu(g_lru)).astype(ylru_ref.dtype)

    qn = q * lax.rsqrt(_segment_sumsq(q, seg_ref) * (1.0 / HEAD_DIM) + EPS) * qg_ref[...]
    kn = k * lax.rsqrt(_segment_sumsq(k, seg_ref) * (1.0 / HEAD_DIM) + EPS) * kg_ref[...]
    q_ref[0] = qn.astype(q_ref.dtype)
    k_ref[0] = kn.astype(k_ref.dtype)
    v_ref[0] = v.astype(v_ref.dtype)


def _inproj_call(x, ln, win, cw, cb, wg, ba, bx, lam, qg, kg, seg):
    B, T, D = x.shape
    tt = min(ROW_TILE, T)
    row_spec = lambda w: pl.BlockSpec((1, tt, w), lambda b, t: (b, t, 0))
    consts = (ln, win, cw, cb, wg, ba, bx, lam, qg, kg, seg)
    return pl.pallas_call(
        _inproj_kernel,
        grid=(B, T // tt),
        in_specs=[row_spec(D)] + [_const_spec(c.shape) for c in consts],
        out_specs=[row_spec(LRU_WIDTH), row_spec(ATT_WIDTH), row_spec(ATT_WIDTH), row_spec(ATT_WIDTH)],
        out_shape=[jax.ShapeDtypeStruct((B, T, LRU_WIDTH), BF16),
                   jax.ShapeDtypeStruct((B, T, ATT_WIDTH), F32),
                   jax.ShapeDtypeStruct((B, T, ATT_WIDTH), F32),
                   jax.ShapeDtypeStruct((B, T, ATT_WIDTH), BF16)],
        scratch_shapes=[pltpu.VMEM((tt + V7X_SUBLANES, LRU_WIDTH), F32),
                        pltpu.VMEM((1, LRU_WIDTH), F32)],
        compiler_params=pltpu.CompilerParams(
            dimension_semantics=("arbitrary", "arbitrary"), vmem_limit_bytes=VMEM_LIMIT),
        name="inproj_rglru",
    )(x, *consts)


def _select_bias(gate, i):
    nb = gate.shape[0]
    rowi = lax.broadcasted_iota(jnp.int32, gate.shape, 0)
    rowf = rowi.astype(F32)
    ninf = -jnp.inf
    g = jnp.where(rowi < i, gate, ninf)
    sel = rowi == i
    for _ in range(min(MOBA_TOPK, i)):
        mx = jnp.max(g, axis=0, keepdims=True)
        cand = jnp.logical_and(g == mx, g > ninf)
        first = jnp.min(jnp.where(cand, rowf, float(nb)), axis=0, keepdims=True)
        pick = jnp.logical_and(rowf == first, cand)
        sel = jnp.logical_or(sel, pick)
        g = jnp.where(pick, ninf, g)
    return jnp.where(sel, 0.0, NEG)


def _attn_kernel(q_ref, k_ref, v_ref, kx_ref, slope_ref, o_ref,
                 kp_ref, vt_ref, km_ref, qp_ref, s_ref, p_ref):
    L = MOBA_BLOCK
    nb = k_ref.shape[1] // L
    W = HEAD_PAIR * HEAD_DIM
    rows8 = (L // V7X_SUBLANES, V7X_SUBLANES, L)

    for j in range(nb):
        kj = k_ref[0, j * L:(j + 1) * L, :]
        kp_ref[j, :, 0:W] = kj.astype(BF16)
        kp_ref[j, :, W:2 * W] = kx_ref[j * L:(j + 1) * L, :]
        km_ref[j:j + 1, :] = jnp.mean(kj, axis=0, keepdims=True)
        vt_ref[:, j * L:(j + 1) * L] = v_ref[0, j * L:(j + 1) * L, :].astype(F32).T.astype(BF16)

    lane = lax.broadcasted_iota(jnp.int32, (L, W), 1)
    causal = lax.broadcasted_iota(jnp.int32, (L, L), 0) <= lax.broadcasted_iota(jnp.int32, (L, L), 1)
    orow = lax.broadcasted_iota(jnp.int32, (W, L), 0)
    pad_rows = jnp.zeros((W - nb - N_ALIBI_ROWS, L), F32)
    qscale = LOG2E / np.sqrt(HEAD_DIM)

    class Stream:
        pass

    def prepare(n, i, hh):
        st = Stream()
        st.i, st.hh, st.slot, st.qslot = i, hh, n % 2, n % qp_ref.shape[0]
        qf = q_ref[0, i * L:(i + 1) * L, :]
        qm = jnp.where(jnp.logical_and(lane >= hh * HEAD_DIM, lane < (hh + 1) * HEAD_DIM), qf, 0.0)
        gate = lax.dot_general(km_ref[...], qm, (((1,), (1,)), ((), ())),
                               precision=lax.Precision.HIGHEST, preferred_element_type=F32)
        qp_ref[st.qslot] = jnp.concatenate([(qm * qscale).T, _select_bias(gate, i), slope_ref[hh], pad_rows],
                                           axis=0).astype(BF16)
        st.m8 = None
        st.l8 = None
        return st

    def score_step(st, c):
        s = _dot(kp_ref[c], qp_ref[st.qslot])
        if c == st.i:
            s = jnp.where(causal, s, NEG)
        s_ref[st.slot, c] = s
        m8 = jnp.max(s.reshape(rows8), axis=0)
        st.m8 = m8 if st.m8 is None else jnp.maximum(st.m8, m8)

    def prob_step(st, c):
        p = jnp.exp2(s_ref[st.slot, c] - st.m)
        l8 = jnp.sum(p.reshape(rows8), axis=0)
        st.l8 = l8 if st.l8 is None else st.l8 + l8
        p_ref[st.slot, c * L:(c + 1) * L, :] = p.astype(BF16)

    def finish(st):
        k_len = (st.i + 1) * L
        acc = _dot(vt_ref[:, 0:k_len], p_ref[st.slot, 0:k_len, :])
        return acc / jnp.sum(st.l8, axis=0, keepdims=True)

    def order(i):
        return [i] + list(range(i))

    streams = [(i, hh) for i in range(nb) for hh in range(HEAD_PAIR)]
    prev = None
    outs = []
    lookahead = qp_ref.shape[0] - 1
    ready = [prepare(n, *streams[n]) for n in range(min(lookahead, len(streams)))]
    for n in range(len(streams) + 1):
        if n + lookahead < len(streams):
            ready.append(prepare(n + lookahead, *streams[n + lookahead]))
        st = ready[n] if n < len(streams) else None
        cur_steps = order(st.i) if st is not None else []
        prev_steps = order(prev.i) if prev is not None else []
        for k in range(max(len(cur_steps), len(prev_steps))):
            if k < len(cur_steps):
                score_step(st, cur_steps[k])
            if k < len(prev_steps):
                prob_step(prev, prev_steps[k])
        if prev is not None:
            outs.append(finish(prev))
            if prev.hh == HEAD_PAIR - 1:
                out_t = outs[HEAD_PAIR - 1]
                for hh in range(HEAD_PAIR - 2, -1, -1):
                    out_t = jnp.where(orow < (hh + 1) * HEAD_DIM, outs[hh], out_t)
                o_ref[0, prev.i * L:(prev.i + 1) * L, :] = out_t.T.astype(o_ref.dtype)
                outs = []
        if st is not None:
            st.m = jnp.max(st.m8, axis=0, keepdims=True)
        prev = st


def _attn_call(q, k, v, kx, slope_rows):
    B, T, _ = q.shape
    L = MOBA_BLOCK
    nb = T // L
    W = HEAD_PAIR * HEAD_DIM
    npairs = N_HEADS // HEAD_PAIR
    seq_spec = pl.BlockSpec((1, T, W), lambda b, g: (b, 0, g))
    return pl.pallas_call(
        _attn_kernel,
        grid=(B, npairs),
        in_specs=[seq_spec, seq_spec, seq_spec, _const_spec(kx.shape),
                  pl.BlockSpec((HEAD_PAIR, N_ALIBI_ROWS, L), lambda b, g: (g, 0, 0))],
        out_specs=seq_spec,
        out_shape=jax.ShapeDtypeStruct((B, T, ATT_WIDTH), BF16),
        scratch_shapes=[pltpu.VMEM((nb, L, 2 * W), BF16),
                        pltpu.VMEM((W, T), BF16),
                        pltpu.VMEM((nb, W), F32),
                        pltpu.VMEM((ATTN_LOOKAHEAD + 1, 2 * W, L), BF16),
                        pltpu.VMEM((2, nb, L, L), F32),
                        pltpu.VMEM((2, T, L), BF16)],
        compiler_params=pltpu.CompilerParams(
            dimension_semantics=("arbitrary", "arbitrary"), vmem_limit_bytes=VMEM_LIMIT),
        name="moba_attention",
    )(q, k, v, kx, slope_rows)


def _ffn_kernel(x_ref, ylru_ref, yatt_ref, p_ref, wout_ref, lnf_ref, wup_ref, fw_ref, fb_ref, wdown_ref,
                lnp_ref, wg_ref, wp_ref, o_ref, gbuf, gcar):
    t = pl.program_id(1)
    tt = x_ref.shape[1]
    d_ff = wdown_ref.shape[0]
    pad = V7X_SUBLANES

    @pl.when(t == 0)
    def _():
        gcar[...] = jnp.zeros_like(gcar)

    h = (x_ref[0] + _dot(ylru_ref[0], wout_ref[0:LRU_WIDTH, :])
         + _dot(yatt_ref[0], wout_ref[LRU_WIDTH:LRU_WIDTH + ATT_WIDTH, :]))
    u = _rmsnorm(h, lnf_ref[...]).astype(BF16)
    acc = jnp.zeros_like(h)
    for c in range(d_ff // FF_CHUNK):
        lo = c * FF_CHUNK
        g = _dot(u, wup_ref[:, lo:lo + FF_CHUNK])
        vv = _dot(u, wup_ref[:, d_ff + lo:d_ff + lo + FF_CHUNK])
        gbuf[0:pad, :] = gcar[c]
        gbuf[pad:pad + tt, :] = g
        gcar[c] = g[tt - pad:tt, :]
        gc = fb_ref[:, lo:lo + FF_CHUNK] + fw_ref[FFN_CONV - 1:FFN_CONV, lo:lo + FF_CHUNK] * g
        for kk in range(FFN_CONV - 1):
            gc = gc + fw_ref[kk:kk + 1, lo:lo + FF_CHUNK] * gbuf[pl.ds(pad - (FFN_CONV - 1) + kk, tt), :]
        act = (jax.nn.gelu(gc) * vv).astype(BF16)
        acc = acc + _dot(act, wdown_ref[lo:lo + FF_CHUNK, :])
    h = h + acc
    gate = jax.nn.sigmoid(_dot(_rmsnorm(h, lnp_ref[...]).astype(BF16), wg_ref[...]))
    o_ref[0] = h + gate * _dot(p_ref[0].astype(BF16), wp_ref[...])


def _ffn_call(x, ylru, yatt, p, wout, lnf, wup, fw, fb, wdown, lnp, wg, wp):
    B, T, D = x.shape
    tt = min(ROW_TILE, T)
    d_ff = wdown.shape[0]
    row_spec = lambda w: pl.BlockSpec((1, tt, w), lambda b, t: (b, t, 0))
    consts = (wout, lnf, wup, fw, fb, wdown, lnp, wg, wp)
    return pl.pallas_call(
        _ffn_kernel,
        grid=(B, T // tt),
        in_specs=[row_spec(D), row_spec(LRU_WIDTH), row_spec(ATT_WIDTH), row_spec(p.shape[-1])]
                 + [_const_spec(c.shape) for c in consts],
        out_specs=row_spec(D),
        out_shape=jax.ShapeDtypeStruct((B, T, D), x.dtype),
        scratch_shapes=[pltpu.VMEM((tt + V7X_SUBLANES, FF_CHUNK), F32),
                        pltpu.VMEM((d_ff // FF_CHUNK, V7X_SUBLANES, FF_CHUNK), F32)],
        compiler_params=pltpu.CompilerParams(
            dimension_semantics=("arbitrary", "arbitrary"), vmem_limit_bytes=VMEM_LIMIT),
        name="outproj_ffn_ple",
    )(x, ylru, yatt, p, *consts)


def _block_diag(w):
    n, c, d = w.shape
    return jnp.einsum("ncd,nm->ncmd", w, jnp.eye(n, dtype=w.dtype)).reshape(n * c, n * d)


def _gate_weights(wa, wx):
    per = V7X_MXU_DIM // LRU_BLOCK_DIM
    groups = []
    for hf in range(LRU_WIDTH // V7X_MXU_DIM):
        sl = slice(hf * per, (hf + 1) * per)
        groups.append(jnp.concatenate([_block_diag(wa[sl]), _block_diag(wx[sl])], axis=1))
    return jnp.stack(groups).astype(BF16)


def _split3_bf16(s):
    s1 = s.astype(BF16).astype(F32)
    s2 = (s - s1).astype(BF16).astype(F32)
    s3 = (s - s1 - s2).astype(BF16).astype(F32)
    return s1, s2, s3


def _attention_constants(T):
    nb = T // MOBA_BLOCK
    W = HEAD_PAIR * HEAD_DIM
    pos = jnp.arange(T, dtype=jnp.int32)
    onehot = (pos[:, None] // MOBA_BLOCK == jnp.arange(nb)[None, :]).astype(F32)
    hi = ((pos >> 6) << 6).astype(F32)[:, None]
    lo = (pos & 63).astype(F32)[:, None]
    cols = [onehot, hi, hi, hi, lo, lo, lo]
    used = nb + 6
    kx = jnp.concatenate(cols + [jnp.zeros((T, W - used), F32)], axis=1).astype(BF16)
    slopes = 2.0 ** (-8.0 * jnp.arange(1, N_HEADS + 1, dtype=F32) / N_HEADS)
    s1, s2, s3 = _split3_bf16(slopes * LOG2E)
    rows = jnp.stack([s1, s2, s3, s1, s2, s3] + [jnp.zeros_like(s1)] * (N_ALIBI_ROWS - 6), axis=1)
    slope_rows = jnp.broadcast_to(rows[:, :, None], (N_HEADS, N_ALIBI_ROWS, MOBA_BLOCK)).astype(F32)
    return kx, slope_rows


def kernel(x, p, ln_mix, w_in, conv_w, conv_b, gate_a_w, gate_a_b, gate_x_w, gate_x_b, lru_lambda, q_gain,
           k_gain, w_out, ln_ffn, w_up, ffn_conv_w, ffn_conv_b, w_down, ln_ple, w_ple_gate, w_ple_proj):
    B, T, D = x.shape
    depth = w_in.shape[0]
    assert T % MOBA_BLOCK == 0 and T % min(ROW_TILE, T) == 0
    row = lambda a: a.reshape(1, -1)
    seg = _block_diag(jnp.ones((V7X_MXU_DIM // HEAD_DIM, HEAD_DIM, HEAD_DIM), F32)).astype(BF16)
    kx, slope_rows = _attention_constants(T)
    h = x
    for i in range(depth):
        ylru, q, k, v = _inproj_call(
            h, row(ln_mix[i]), w_in[i].astype(BF16), conv_w[i], row(conv_b[i]),
            _gate_weights(gate_a_w[i], gate_x_w[i]), row(gate_a_b[i]), row(gate_x_b[i]), row(lru_lambda[i]),
            row(jnp.tile(q_gain[i], N_HEADS)), row(jnp.tile(k_gain[i], N_HEADS)), seg)
        yatt = _attn_call(q, k, v, kx, slope_rows)
        h = _ffn_call(
            h, ylru, yatt, p[i].astype(h.dtype), w_out[i].astype(BF16), row(ln_ffn[i]), w_up[i].astype(BF16),
            ffn_conv_w[i], row(ffn_conv_b[i]), w_down[i].astype(BF16), row(ln_ple[i]),
            w_ple_gate[i].astype(BF16), w_ple_proj[i].astype(BF16))
    return h
```

```python
import functools

import numpy as np
import jax
import jax.numpy as jnp
from jax import lax
from jax.experimental import pallas as pl
from jax.experimental.pallas import tpu as pltpu

F32 = jnp.float32
BF16 = jnp.bfloat16

LRU_WIDTH = 512
LRU_BLOCKS = 8
LRU_BLOCK_DIM = LRU_WIDTH // LRU_BLOCKS
LRU_CONV = 4
LRU_C = 8.0
N_HEADS = 8
HEAD_DIM = 64
ATT_WIDTH = N_HEADS * HEAD_DIM
MOBA_BLOCK = 256
MOBA_TOPK = 3
FFN_CONV = 3
EPS = 1e-6
NEG = -1e30
LOG2E = float(np.log2(np.e))

V7X_LANES = 128
V7X_SUBLANES = 8
V7X_MXU_DIM = 256
V7X_VMEM_BYTES = 64 * 1024 * 1024

ROW_TILE = 512
HEAD_PAIR = V7X_LANES // HEAD_DIM
FF_CHUNK = 1024
N_ALIBI_ROWS = V7X_SUBLANES
ONES_ROWS = 16
ATTN_LOOKAHEAD = 2
VMEM_LIMIT = 56 * 1024 * 1024


def _const_spec(shape):
    nd = len(shape)
    return pl.BlockSpec(shape, lambda *_: (0,) * nd, pipeline_mode=pl.Buffered(1))


def _rmsnorm(x, g):
    return x * lax.rsqrt(jnp.mean(x * x, axis=-1, keepdims=True) + EPS) * g


def _dot(a, b):
    return jnp.dot(a, b, preferred_element_type=F32)


def _segment_sumsq(z, seg_ref):
    sq = z * z
    parts = []
    for hf in range(z.shape[1] // V7X_MXU_DIM):
        s = sq[:, hf * V7X_MXU_DIM:(hf + 1) * V7X_MXU_DIM]
        hi = s.astype(BF16)
        lo = (s - hi.astype(F32)).astype(BF16)
        parts.append(_dot(hi, seg_ref[...]) + _dot(lo, seg_ref[...]))
    return jnp.concatenate(parts, axis=1)


def _linear_scan(a, b, h_in):
    n, c = a.shape
    sl = V7X_SUBLANES
    a3 = a.reshape(n // sl, sl, c)
    b3 = b.reshape(n // sl, sl, c)
    sub = lax.broadcasted_iota(jnp.int32, a3.shape, 1)
    s = 1
    while s < sl:
        keep = sub >= s
        a_sh = jnp.where(keep, pltpu.roll(a3, s, 1), 1.0)
        b_sh = jnp.where(keep, pltpu.roll(b3, s, 1), 0.0)
        b3 = b3 + a3 * b_sh
        a3 = a3 * a_sh
        s *= 2
    carry = h_in
    groups = []
    for g in range(n // sl):
        hg = b3[g] + a3[g] * carry
        groups.append(hg)
        carry = hg[sl - 1:sl, :]
    return jnp.concatenate(groups, axis=0)


def _inproj_kernel(x_ref, ln_ref, win_ref, cw_ref, cb_ref, wg_ref, ba_ref, bx_ref, lam_ref,
                   qg_ref, kg_ref, seg_ref,
                   ylru_ref, q_ref, k_ref, v_ref,
                   xbuf, hcarry):
    t = pl.program_id(1)
    tt = x_ref.shape[1]

    @pl.when(t == 0)
    def _():
        xbuf[0:V7X_SUBLANES, :] = jnp.zeros((V7X_SUBLANES, LRU_WIDTH), F32)
        hcarry[...] = jnp.zeros_like(hcarry)

    u = _rmsnorm(x_ref[0], ln_ref[...])
    proj = _dot(u.astype(BF16), win_ref[...])
    x_lru = proj[:, 0:LRU_WIDTH]
    g_lru = proj[:, LRU_WIDTH:2 * LRU_WIDTH]
    o = 2 * LRU_WIDTH
    q = proj[:, o:o + ATT_WIDTH]
    k = proj[:, o + ATT_WIDTH:o + 2 * ATT_WIDTH]
    v = proj[:, o + 2 * ATT_WIDTH:o + 3 * ATT_WIDTH]

    xbuf[V7X_SUBLANES:V7X_SUBLANES + tt, :] = x_lru
    xc = cb_ref[...] + cw_ref[LRU_CONV - 1:LRU_CONV, :] * x_lru
    for kk in range(LRU_CONV - 1):
        start = V7X_SUBLANES - (LRU_CONV - 1) + kk
        xc = xc + cw_ref[kk:kk + 1, :] * xbuf[pl.ds(start, tt), :]
    xbuf[0:V7X_SUBLANES, :] = x_lru[tt - V7X_SUBLANES:tt, :]

    xcb = xc.astype(BF16)
    r_parts, i_parts = [], []
    for hf in range(LRU_WIDTH // V7X_MXU_DIM):
        z = _dot(xcb[:, hf * V7X_MXU_DIM:(hf + 1) * V7X_MXU_DIM], wg_ref[hf])
        r_parts.append(z[:, 0:V7X_MXU_DIM])
        i_parts.append(z[:, V7X_MXU_DIM:2 * V7X_MXU_DIM])
    r = jax.nn.sigmoid(jnp.concatenate(r_parts, axis=1) + ba_ref[...])
    ig = jax.nn.sigmoid(jnp.concatenate(i_parts, axis=1) + bx_ref[...])

    nl = -lam_ref[...]
    softplus = jnp.maximum(nl, 0.0) + jnp.log(1.0 + jnp.exp(-jnp.abs(nl)))
    log_a = (-LRU_C) * r * softplus
    a = jnp.exp(log_a)
    z = 1.0 - a * a
    mult = jnp.where(z > 0.0, z * lax.rsqrt(z), 0.0)
    gx = ig * xc
    b = mult * gx
    first = lax.broadcasted_iota(jnp.int32, (V7X_SUBLANES, LRU_WIDTH), 0) + t * tt == 0
    b = jnp.concatenate([jnp.where(first, gx[0:V7X_SUBLANES, :], b[0:V7X_SUBLANES, :]), b[V7X_SUBLANES:, :]], axis=0)

    h = _linear_scan(a, b, hcarry[...])
    hcarry[...] = h[tt - 1:tt, :]
    ylru_ref[0] = (h * jax.nn.gelu(g_lru)).astype(ylru_ref.dtype)

    qn = q * lax.rsqrt(_segment_sumsq(q, seg_ref) * (1.0 / HEAD_DIM) + EPS) * qg_ref[...]
    kn = k * lax.rsqrt(_segment_sumsq(k, seg_ref) * (1.0 / HEAD_DIM) + EPS) * kg_ref[...]
    q_ref[0] = qn.astype(q_ref.dtype)
    k_ref[0] = kn.astype(k_ref.dtype)
    v_ref[0] = v.astype(v_ref.dtype)


def _inproj_call(x, ln, win, cw, cb, wg, ba, bx, lam, qg, kg, seg):
    B, T, D = x.shape
    tt = min(ROW_TILE, T)
    row_spec = lambda w: pl.BlockSpec((1, tt, w), lambda b, t: (b, t, 0))
    consts = (ln, win, cw, cb, wg, ba, bx, lam, qg, kg, seg)
    return pl.pallas_call(
        _inproj_kernel,
        grid=(B, T // tt),
        in_specs=[row_spec(D)] + [_const_spec(c.shape) for c in consts],
        out_specs=[row_spec(LRU_WIDTH), row_spec(ATT_WIDTH), row_spec(ATT_WIDTH), row_spec(ATT_WIDTH)],
        out_shape=[jax.ShapeDtypeStruct((B, T, LRU_WIDTH), BF16),
                   jax.ShapeDtypeStruct((B, T, ATT_WIDTH), F32),
                   jax.ShapeDtypeStruct((B, T, ATT_WIDTH), F32),
                   jax.ShapeDtypeStruct((B, T, ATT_WIDTH), BF16)],
        scratch_shapes=[pltpu.VMEM((tt + V7X_SUBLANES, LRU_WIDTH), F32),
                        pltpu.VMEM((1, LRU_WIDTH), F32)],
        compiler_params=pltpu.CompilerParams(
            dimension_semantics=("arbitrary", "arbitrary"), vmem_limit_bytes=VMEM_LIMIT),
        name="inproj_rglru",
    )(x, *consts)


def _select_bias(gate, i):
    nb = gate.shape[0]
    rowi = lax.broadcasted_iota(jnp.int32, gate.shape, 0)
    rowf = rowi.astype(F32)
    ninf = -jnp.inf
    g = jnp.where(rowi < i, gate, ninf)
    sel = rowi == i
    for _ in range(min(MOBA_TOPK, i)):
        mx = jnp.max(g, axis=0, keepdims=True)
        cand = jnp.logical_and(g == mx, g > ninf)
        first = jnp.min(jnp.where(cand, rowf, float(nb)), axis=0, keepdims=True)
        pick = jnp.logical_and(rowf == first, cand)
        sel = jnp.logical_or(sel, pick)
        g = jnp.where(pick, ninf, g)
    return jnp.where(sel, 0.0, NEG)


def _attn_kernel(q_ref, k_ref, v_ref, kx_ref, slope_ref, zero_ref, o_ref,
                 kp_ref, vt_ref, km_ref, qp_ref, s_ref, p_ref):
    L = MOBA_BLOCK
    nb = k_ref.shape[1] // L
    W = HEAD_PAIR * HEAD_DIM
    rows8 = (L // V7X_SUBLANES, V7X_SUBLANES, L)

    for j in range(nb):
        kj = k_ref[0, j * L:(j + 1) * L, :]
        kp_ref[j, :, 0:W] = kj.astype(BF16)
        kp_ref[j, :, W:2 * W] = kx_ref[j * L:(j + 1) * L, :]
        km_ref[j:j + 1, :] = jnp.mean(kj, axis=0, keepdims=True)
        vt = v_ref[0, j * L:(j + 1) * L, :].astype(F32).T.astype(BF16)
        for hh in range(HEAD_PAIR):
            vt_ref[hh, 0:HEAD_DIM, j * L:(j + 1) * L] = vt[hh * HEAD_DIM:(hh + 1) * HEAD_DIM, :]
    for hh in range(HEAD_PAIR):
        vt_ref[hh, HEAD_DIM:HEAD_DIM + ONES_ROWS, :] = jnp.ones((ONES_ROWS, nb * L), BF16)

    lane = lax.broadcasted_iota(jnp.int32, (L, W), 1)
    causal = lax.broadcasted_iota(jnp.int32, (L, L), 0) <= lax.broadcasted_iota(jnp.int32, (L, L), 1)
    pad_rows = jnp.zeros((W - nb - N_ALIBI_ROWS, L), F32)
    qscale = LOG2E / np.sqrt(HEAD_DIM)

    dyn0 = zero_ref[0]

    class Stream:
        pass

    def prepare(n, i, hh):
        st = Stream()
        st.i, st.hh, st.slot, st.qslot = i, hh, n % 2, n % qp_ref.shape[0]
        qf = q_ref[0, i * L:(i + 1) * L, :]
        qm = jnp.where(jnp.logical_and(lane >= hh * HEAD_DIM, lane < (hh + 1) * HEAD_DIM), qf, 0.0)
        gate = lax.dot_general(km_ref[...], qm, (((1,), (1,)), ((), ())),
                               precision=lax.Precision.HIGHEST, preferred_element_type=F32)
        qp_ref[st.qslot] = jnp.concatenate([(qm * qscale).T, _select_bias(gate, i), slope_ref[hh], pad_rows],
                                           axis=0).astype(BF16)
        st.m8 = None
        return st

    def score_step(st, c):
        s = _dot(kp_ref[c], qp_ref[st.qslot])
        if c == st.i:
            s = jnp.where(causal, s, NEG)
        s_ref[st.slot + dyn0, c] = s
        m8 = jnp.max(s.reshape(rows8), axis=0)
        st.m8 = m8 if st.m8 is None else jnp.maximum(st.m8, m8)

    def prob_step(st, c):
        p_ref[st.slot, c * L:(c + 1) * L, :] = jnp.exp2(s_ref[st.slot + dyn0, c] - st.m).astype(BF16)

    def finish(st):
        k_len = (st.i + 1) * L
        acc = _dot(vt_ref[st.hh, :, 0:k_len], p_ref[st.slot, 0:k_len, :])
        return acc[0:HEAD_DIM, :] / acc[HEAD_DIM:HEAD_DIM + 1, :]

    def order(i):
        return [i] + list(range(i))

    streams = [(i, hh) for i in range(nb) for hh in range(HEAD_PAIR)]
    outs = []
    lookahead = qp_ref.shape[0] - 1
    ready = [prepare(n, *streams[n]) for n in range(min(lookahead, len(streams)))]

    def emit_pv(st):
        outs.append(finish(st))
        if st.hh == HEAD_PAIR - 1:
            o_ref[0, st.i * L:(st.i + 1) * L, :] = jnp.concatenate(outs, axis=0).T.astype(o_ref.dtype)
            outs.clear()

    for n in range(len(streams) + 2):
        if n + lookahead < len(streams):
            ready.append(prepare(n + lookahead, *streams[n + lookahead]))
        st = ready[n] if n < len(streams) else None
        pr = ready[n - 1] if 0 <= n - 1 < len(streams) else None
        pv = ready[n - 2] if 0 <= n - 2 < len(streams) else None
        cur_steps = order(st.i) if st is not None else []
        prev_steps = order(pr.i) if pr is not None else []
        n_steps = max(len(cur_steps), len(prev_steps))
        for k in range(n_steps):
            if pv is not None and k == n_steps // 2:
                emit_pv(pv)
                pv = None
            if k < len(cur_steps):
                score_step(st, cur_steps[k])
            if k < len(prev_steps):
                prob_step(pr, prev_steps[k])
        if pv is not None:
            emit_pv(pv)
        if st is not None:
            st.m = jnp.max(st.m8, axis=0, keepdims=True)


def _attn_call(q, k, v, kx, slope_rows):
    B, T, _ = q.shape
    L = MOBA_BLOCK
    nb = T // L
    W = HEAD_PAIR * HEAD_DIM
    npairs = N_HEADS // HEAD_PAIR
    seq_spec = pl.BlockSpec((1, T, W), lambda b, g: (b, 0, g))
    return pl.pallas_call(
        _attn_kernel,
        grid=(B, npairs),
        in_specs=[seq_spec, seq_spec, seq_spec, _const_spec(kx.shape),
                  pl.BlockSpec((HEAD_PAIR, N_ALIBI_ROWS, L), lambda b, g: (g, 0, 0)),
                  pl.BlockSpec(memory_space=pltpu.SMEM)],
        out_specs=seq_spec,
        out_shape=jax.ShapeDtypeStruct((B, T, ATT_WIDTH), BF16),
        scratch_shapes=[pltpu.VMEM((nb, L, 2 * W), BF16),
                        pltpu.VMEM((HEAD_PAIR, HEAD_DIM + ONES_ROWS, T), BF16),
                        pltpu.VMEM((nb, W), F32),
                        pltpu.VMEM((ATTN_LOOKAHEAD + 1, 2 * W, L), BF16),
                        pltpu.VMEM((2, nb, L, L), F32),
                        pltpu.VMEM((2, T, L), BF16)],
        compiler_params=pltpu.CompilerParams(
            dimension_semantics=("arbitrary", "arbitrary"), vmem_limit_bytes=VMEM_LIMIT),
        name="moba_attention",
    )(q, k, v, kx, slope_rows, jnp.zeros((1,), jnp.int32))


def _ffn_kernel(x_ref, ylru_ref, yatt_ref, p_ref, wout_ref, lnf_ref, wup_ref, fw_ref, fb_ref, wdown_ref,
                lnp_ref, wg_ref, wp_ref, o_ref, gbuf, gcar):
    t = pl.program_id(1)
    tt = x_ref.shape[1]
    d_ff = wdown_ref.shape[0]
    pad = V7X_SUBLANES

    @pl.when(t == 0)
    def _():
        gcar[...] = jnp.zeros_like(gcar)

    h = (x_ref[0] + _dot(ylru_ref[0], wout_ref[0:LRU_WIDTH, :])
         + _dot(yatt_ref[0], wout_ref[LRU_WIDTH:LRU_WIDTH + ATT_WIDTH, :]))
    u = _rmsnorm(h, lnf_ref[...]).astype(BF16)
    acc = jnp.zeros_like(h)
    for c in range(d_ff // FF_CHUNK):
        lo = c * FF_CHUNK
        g = _dot(u, wup_ref[:, lo:lo + FF_CHUNK])
        vv = _dot(u, wup_ref[:, d_ff + lo:d_ff + lo + FF_CHUNK])
        gbuf[0:pad, :] = gcar[c]
        gbuf[pad:pad + tt, :] = g
        gcar[c] = g[tt - pad:tt, :]
        gc = fb_ref[:, lo:lo + FF_CHUNK] + fw_ref[FFN_CONV - 1:FFN_CONV, lo:lo + FF_CHUNK] * g
        for kk in range(FFN_CONV - 1):
            gc = gc + fw_ref[kk:kk + 1, lo:lo + FF_CHUNK] * gbuf[pl.ds(pad - (FFN_CONV - 1) + kk, tt), :]
        act = (jax.nn.gelu(gc) * vv).astype(BF16)
        acc = acc + _dot(act, wdown_ref[lo:lo + FF_CHUNK, :])
    h = h + acc
    gate = jax.nn.sigmoid(_dot(_rmsnorm(h, lnp_ref[...]).astype(BF16), wg_ref[...]))
    o_ref[0] = h + gate * _dot(p_ref[0].astype(BF16), wp_ref[...])


def _ffn_call(x, ylru, yatt, p, wout, lnf, wup, fw, fb, wdown, lnp, wg, wp):
    B, T, D = x.shape
    tt = min(ROW_TILE, T)
    d_ff = wdown.shape[0]
    row_spec = lambda w: pl.BlockSpec((1, tt, w), lambda b, t: (b, t, 0))
    consts = (wout, lnf, wup, fw, fb, wdown, lnp, wg, wp)
    return pl.pallas_call(
        _ffn_kernel,
        grid=(B, T // tt),
        in_specs=[row_spec(D), row_spec(LRU_WIDTH), row_spec(ATT_WIDTH), row_spec(p.shape[-1])]
                 + [_const_spec(c.shape) for c in consts],
        out_specs=row_spec(D),
        out_shape=jax.ShapeDtypeStruct((B, T, D), x.dtype),
        scratch_shapes=[pltpu.VMEM((tt + V7X_SUBLANES, FF_CHUNK), F32),
                        pltpu.VMEM((d_ff // FF_CHUNK, V7X_SUBLANES, FF_CHUNK), F32)],
        compiler_params=pltpu.CompilerParams(
            dimension_semantics=("arbitrary", "arbitrary"), vmem_limit_bytes=VMEM_LIMIT),
        name="outproj_ffn_ple",
    )(x, ylru, yatt, p, *consts)


def _block_diag(w):
    n, c, d = w.shape
    return jnp.einsum("ncd,nm->ncmd", w, jnp.eye(n, dtype=w.dtype)).reshape(n * c, n * d)


def _gate_weights(wa, wx):
    per = V7X_MXU_DIM // LRU_BLOCK_DIM
    groups = []
    for hf in range(LRU_WIDTH // V7X_MXU_DIM):
        sl = slice(hf * per, (hf + 1) * per)
        groups.append(jnp.concatenate([_block_diag(wa[sl]), _block_diag(wx[sl])], axis=1))
    return jnp.stack(groups).astype(BF16)


def _split3_bf16(s):
    s1 = s.astype(BF16).astype(F32)
    s2 = (s - s1).astype(BF16).astype(F32)
    s3 = (s - s1 - s2).astype(BF16).astype(F32)
    return s1, s2, s3


def _attention_constants(T):
    nb = T // MOBA_BLOCK
    W = HEAD_PAIR * HEAD_DIM
    pos = jnp.arange(T, dtype=jnp.int32)
    onehot = (pos[:, None] // MOBA_BLOCK == jnp.arange(nb)[None, :]).astype(F32)
    hi = ((pos >> 6) << 6).astype(F32)[:, None]
    lo = (pos & 63).astype(F32)[:, None]
    cols = [onehot, hi, hi, hi, lo, lo, lo]
    used = nb + 6
    kx = jnp.concatenate(cols + [jnp.zeros((T, W - used), F32)], axis=1).astype(BF16)
    slopes = 2.0 ** (-8.0 * jnp.arange(1, N_HEADS + 1, dtype=F32) / N_HEADS)
    s1, s2, s3 = _split3_bf16(slopes * LOG2E)
    rows = jnp.stack([s1, s2, s3, s1, s2, s3] + [jnp.zeros_like(s1)] * (N_ALIBI_ROWS - 6), axis=1)
    slope_rows = jnp.broadcast_to(rows[:, :, None], (N_HEADS, N_ALIBI_ROWS, MOBA_BLOCK)).astype(F32)
    return kx, slope_rows


def kernel(x, p, ln_mix, w_in, conv_w, conv_b, gate_a_w, gate_a_b, gate_x_w, gate_x_b, lru_lambda, q_gain,
           k_gain, w_out, ln_ffn, w_up, ffn_conv_w, ffn_conv_b, w_down, ln_ple, w_ple_gate, w_ple_proj):
    B, T, D = x.shape
    depth = w_in.shape[0]
    assert T % MOBA_BLOCK == 0 and T % min(ROW_TILE, T) == 0
    row = lambda a: a.reshape(1, -1)
    seg = _block_diag(jnp.ones((V7X_MXU_DIM // HEAD_DIM, HEAD_DIM, HEAD_DIM), F32)).astype(BF16)
    kx, slope_rows = _attention_constants(T)
    h = x
    for i in range(depth):
        ylru, q, k, v = _inproj_call(
            h, row(ln_mix[i]), w_in[i].astype(BF16), conv_w[i], row(conv_b[i]),
            _gate_weights(gate_a_w[i], gate_x_w[i]), row(gate_a_b[i]), row(gate_x_b[i]), row(lru_lambda[i]),
            row(jnp.tile(q_gain[i], N_HEADS)), row(jnp.tile(k_gain[i], N_HEADS)), seg)
        yatt = _attn_call(q, k, v, kx, slope_rows)
        h = _ffn_call(
            h, ylru, yatt, p[i].astype(h.dtype), w_out[i].astype(BF16), row(ln_ffn[i]), w_up[i].astype(BF16),
            ffn_conv_w[i], row(ffn_conv_b[i]), w_down[i].astype(BF16), row(ln_ple[i]),
            w_ple_gate[i].astype(BF16), w_ple_proj[i].astype(BF16))
    return h
```

```python
import functools

import numpy as np
import jax
import jax.numpy as jnp
from jax import lax
from jax.experimental import pallas as pl
from jax.experimental.pallas import tpu as pltpu

F32 = jnp.float32
BF16 = jnp.bfloat16

LRU_WIDTH = 512
LRU_BLOCKS = 8
LRU_BLOCK_DIM = LRU_WIDTH // LRU_BLOCKS
LRU_CONV = 4
LRU_C = 8.0
N_HEADS = 8
HEAD_DIM = 64
ATT_WIDTH = N_HEADS * HEAD_DIM
MOBA_BLOCK = 256
MOBA_TOPK = 3
FFN_CONV = 3
EPS = 1e-6
NEG = -1e30
LOG2E = float(np.log2(np.e))

V7X_LANES = 128
V7X_SUBLANES = 8
V7X_MXU_DIM = 256
V7X_VMEM_BYTES = 64 * 1024 * 1024

ROW_TILE = 512
HEAD_PAIR = V7X_LANES // HEAD_DIM
INPROJ_SUB_ROWS = 128
FF_CHUNK = 1536
FFN_SUB_ROWS = 256
N_ALIBI_ROWS = V7X_SUBLANES
ONES_ROWS = 16
ATTN_LOOKAHEAD = 2
VMEM_LIMIT = 56 * 1024 * 1024


def _const_spec(shape):
    nd = len(shape)
    return pl.BlockSpec(shape, lambda *_: (0,) * nd, pipeline_mode=pl.Buffered(1))


def _rmsnorm(x, g):
    return x * lax.rsqrt(jnp.mean(x * x, axis=-1, keepdims=True) + EPS) * g


def _dot(a, b):
    return jnp.dot(a, b, preferred_element_type=F32)


def _run_skewed(chains, n_phase):
    for step in range(n_phase + len(chains) - 1):
        for r, ch in enumerate(chains):
            if 0 <= step - r < n_phase:
                next(ch)


def _segment_sumsq(z, seg_ref):
    sq = z * z
    parts = []
    for hf in range(z.shape[1] // V7X_MXU_DIM):
        s = sq[:, hf * V7X_MXU_DIM:(hf + 1) * V7X_MXU_DIM]
        hi = s.astype(BF16)
        lo = (s - hi.astype(F32)).astype(BF16)
        parts.append(_dot(hi, seg_ref[...]) + _dot(lo, seg_ref[...]))
    return jnp.concatenate(parts, axis=1)


def _linear_scan(a, b, h_in):
    n, c = a.shape
    sl = V7X_SUBLANES
    a3 = a.reshape(n // sl, sl, c)
    b3 = b.reshape(n // sl, sl, c)
    sub = lax.broadcasted_iota(jnp.int32, a3.shape, 1)
    s = 1
    while s < sl:
        keep = sub >= s
        a_sh = jnp.where(keep, pltpu.roll(a3, s, 1), 1.0)
        b_sh = jnp.where(keep, pltpu.roll(b3, s, 1), 0.0)
        b3 = b3 + a3 * b_sh
        a3 = a3 * a_sh
        s *= 2
    carry = h_in
    groups = []
    for g in range(n // sl):
        hg = b3[g] + a3[g] * carry
        groups.append(hg)
        carry = hg[sl - 1:sl, :]
    return jnp.concatenate(groups, axis=0)


def _inproj_kernel(x_ref, ln_ref, win_ref, cw_ref, cb_ref, wg_ref, ba_ref, bx_ref, lam_ref,
                   qg_ref, kg_ref, seg_ref,
                   ylru_ref, q_ref, k_ref, v_ref,
                   xbuf, hcarry):
    t = pl.program_id(1)
    tt = x_ref.shape[1]

    @pl.when(t == 0)
    def _():
        xbuf[0:V7X_SUBLANES, :] = jnp.zeros((V7X_SUBLANES, LRU_WIDTH), F32)
        hcarry[...] = jnp.zeros_like(hcarry)

    nl = -lam_ref[...]
    softplus = jnp.maximum(nl, 0.0) + jnp.log(1.0 + jnp.exp(-jnp.abs(nl)))
    sub = INPROJ_SUB_ROWS
    n_groups = win_ref.shape[1] // LRU_WIDTH
    carry = [hcarry[...]]

    def chain(r):
        rows = slice(r * sub, (r + 1) * sub)
        box = {}

        def project():
            u = _rmsnorm(x_ref[0, rows, :], ln_ref[...]).astype(BF16)
            yield
            parts = []
            for g in range(n_groups):
                parts.append(_dot(u, win_ref[:, g * LRU_WIDTH:(g + 1) * LRU_WIDTH]))
                yield
            box["parts"] = parts

        def tail():
            x_lru, g_lru, q, k, v = box["parts"]
            v_ref[0, rows, :] = v.astype(v_ref.dtype)
            base = V7X_SUBLANES + r * sub
            xbuf[base:base + sub, :] = x_lru
            xc = cb_ref[...] + cw_ref[LRU_CONV - 1:LRU_CONV, :] * x_lru
            for kk in range(LRU_CONV - 1):
                xc = xc + cw_ref[kk:kk + 1, :] * xbuf[pl.ds(base - (LRU_CONV - 1) + kk, sub), :]
            yield
            xcb = xc.astype(BF16)
            r_parts, i_parts = [], []
            for hf in range(LRU_WIDTH // V7X_MXU_DIM):
                z = _dot(xcb[:, hf * V7X_MXU_DIM:(hf + 1) * V7X_MXU_DIM], wg_ref[hf])
                r_parts.append(z[:, 0:V7X_MXU_DIM])
                i_parts.append(z[:, V7X_MXU_DIM:2 * V7X_MXU_DIM])
            rg = jax.nn.sigmoid(jnp.concatenate(r_parts, axis=1) + ba_ref[...])
            ig = jax.nn.sigmoid(jnp.concatenate(i_parts, axis=1) + bx_ref[...])
            a = jnp.exp((-LRU_C) * rg * softplus)
            z = 1.0 - a * a
            mult = jnp.where(z > 0.0, z * lax.rsqrt(z), 0.0)
            gx = ig * xc
            b = mult * gx
            if r == 0:
                first = lax.broadcasted_iota(jnp.int32, (V7X_SUBLANES, LRU_WIDTH), 0) + t * tt == 0
                b = jnp.concatenate([jnp.where(first, gx[0:V7X_SUBLANES, :], b[0:V7X_SUBLANES, :]),
                                     b[V7X_SUBLANES:, :]], axis=0)
            yield
            h = _linear_scan(a, b, carry[0])
            carry[0] = h[sub - 1:sub, :]
            ylru_ref[0, rows, :] = (h * jax.nn.gelu(g_lru)).astype(ylru_ref.dtype)
            yield
            qn = q * lax.rsqrt(_segment_sumsq(q, seg_ref) * (1.0 / HEAD_DIM) + EPS) * qg_ref[...]
            q_ref[0, rows, :] = qn.astype(q_ref.dtype)
            yield
            kn = k * lax.rsqrt(_segment_sumsq(k, seg_ref) * (1.0 / HEAD_DIM) + EPS) * kg_ref[...]
            k_ref[0, rows, :] = kn.astype(k_ref.dtype)
            yield

        return project(), tail()

    def drain(gen):
        for _ in gen:
            pass

    pairs = [chain(r) for r in range(tt // sub)]
    drain(pairs[0][0])
    for r in range(1, len(pairs)):
        proj, prev_tail = pairs[r][0], pairs[r - 1][1]
        next(proj)
        while True:
            proj_done = next(proj, "done") == "done"
            tail_done = next(prev_tail, "done") == "done"
            if proj_done and tail_done:
                break
    drain(pairs[-1][1])
    xbuf[0:V7X_SUBLANES, :] = xbuf[tt:tt + V7X_SUBLANES, :]
    hcarry[...] = carry[0]


def _inproj_call(x, ln, win, cw, cb, wg, ba, bx, lam, qg, kg, seg):
    B, T, D = x.shape
    tt = min(ROW_TILE, T)
    row_spec = lambda w: pl.BlockSpec((1, tt, w), lambda b, t: (b, t, 0))
    consts = (ln, win, cw, cb, wg, ba, bx, lam, qg, kg, seg)
    return pl.pallas_call(
        _inproj_kernel,
        grid=(B, T // tt),
        in_specs=[row_spec(D)] + [_const_spec(c.shape) for c in consts],
        out_specs=[row_spec(LRU_WIDTH), row_spec(ATT_WIDTH), row_spec(ATT_WIDTH), row_spec(ATT_WIDTH)],
        out_shape=[jax.ShapeDtypeStruct((B, T, LRU_WIDTH), BF16),
                   jax.ShapeDtypeStruct((B, T, ATT_WIDTH), F32),
                   jax.ShapeDtypeStruct((B, T, ATT_WIDTH), F32),
                   jax.ShapeDtypeStruct((B, T, ATT_WIDTH), BF16)],
        scratch_shapes=[pltpu.VMEM((tt + V7X_SUBLANES, LRU_WIDTH), F32),
                        pltpu.VMEM((1, LRU_WIDTH), F32)],
        compiler_params=pltpu.CompilerParams(
            dimension_semantics=("arbitrary", "arbitrary"), vmem_limit_bytes=VMEM_LIMIT),
        name="inproj_rglru",
    )(x, *consts)


def _select_bias(gate, i):
    nb = gate.shape[0]
    rowi = lax.broadcasted_iota(jnp.int32, gate.shape, 0)
    rowf = rowi.astype(F32)
    ninf = -jnp.inf
    g = jnp.where(rowi < i, gate, ninf)
    sel = rowi == i
    for _ in range(min(MOBA_TOPK, i)):
        mx = jnp.max(g, axis=0, keepdims=True)
        cand = jnp.logical_and(g == mx, g > ninf)
        first = jnp.min(jnp.where(cand, rowf, float(nb)), axis=0, keepdims=True)
        pick = jnp.logical_and(rowf == first, cand)
        sel = jnp.logical_or(sel, pick)
        g = jnp.where(pick, ninf, g)
    return jnp.where(sel, 0.0, NEG)


def _attn_kernel(q_ref, k_ref, v_ref, kx_ref, slope_ref, zero_ref, o_ref,
                 kp_ref, vt_ref, km_ref, qp_ref, s_ref, p_ref):
    L = MOBA_BLOCK
    nb = k_ref.shape[1] // L
    W = HEAD_PAIR * HEAD_DIM
    rows8 = (L // V7X_SUBLANES, V7X_SUBLANES, L)

    for j in range(nb):
        kj = k_ref[0, j * L:(j + 1) * L, :]
        kp_ref[j, :, 0:W] = kj.astype(BF16)
        kp_ref[j, :, W:2 * W] = kx_ref[j * L:(j + 1) * L, :]
        km_ref[j:j + 1, :] = jnp.mean(kj, axis=0, keepdims=True)
        vt = v_ref[0, j * L:(j + 1) * L, :].astype(F32).T.astype(BF16)
        for hh in range(HEAD_PAIR):
            vt_ref[hh, 0:HEAD_DIM, j * L:(j + 1) * L] = vt[hh * HEAD_DIM:(hh + 1) * HEAD_DIM, :]
    for hh in range(HEAD_PAIR):
        vt_ref[hh, HEAD_DIM:HEAD_DIM + ONES_ROWS, :] = jnp.ones((ONES_ROWS, nb * L), BF16)

    lane = lax.broadcasted_iota(jnp.int32, (L, W), 1)
    causal = lax.broadcasted_iota(jnp.int32, (L, L), 0) <= lax.broadcasted_iota(jnp.int32, (L, L), 1)
    pad_rows = jnp.zeros((W - nb - N_ALIBI_ROWS, L), F32)
    qscale = LOG2E / np.sqrt(HEAD_DIM)

    dyn0 = zero_ref[0]

    class Stream:
        pass

    def prepare(n, i, hh):
        st = Stream()
        st.i, st.hh, st.slot, st.qslot = i, hh, n % 2, n % qp_ref.shape[0]
        qf = q_ref[0, i * L:(i + 1) * L, :]
        qm = jnp.where(jnp.logical_and(lane >= hh * HEAD_DIM, lane < (hh + 1) * HEAD_DIM), qf, 0.0)
        gate = lax.dot_general(km_ref[...], qm, (((1,), (1,)), ((), ())),
                               precision=lax.Precision.HIGHEST, preferred_element_type=F32)
        qp_ref[st.qslot] = jnp.concatenate([(qm * qscale).T, _select_bias(gate, i), slope_ref[hh], pad_rows],
                                           axis=0).astype(BF16)
        st.m8 = None
        return st

    def score_step(st, c):
        s = _dot(kp_ref[c], qp_ref[st.qslot])
        if c == st.i:
            s = jnp.where(causal, s, NEG)
        s_ref[st.slot + dyn0, c] = s
        m8 = jnp.max(s.reshape(rows8), axis=0)
        st.m8 = m8 if st.m8 is None else jnp.maximum(st.m8, m8)

    def prob_step(st, c):
        p_ref[st.slot, c * L:(c + 1) * L, :] = jnp.exp2(s_ref[st.slot + dyn0, c] - st.m).astype(BF16)

    def finish(st):
        k_len = (st.i + 1) * L
        acc = _dot(vt_ref[st.hh, :, 0:k_len], p_ref[st.slot, 0:k_len, :])
        return acc[0:HEAD_DIM, :] / acc[HEAD_DIM:HEAD_DIM + 1, :]

    def order(i):
        return [i] + list(range(i))

    streams = [(i, hh) for i in range(nb) for hh in range(HEAD_PAIR)]
    outs = []
    lookahead = qp_ref.shape[0] - 1
    ready = [prepare(n, *streams[n]) for n in range(min(lookahead, len(streams)))]

    def emit_pv(st):
        outs.append(finish(st))
        if st.hh == HEAD_PAIR - 1:
            o_ref[0, st.i * L:(st.i + 1) * L, :] = jnp.concatenate(outs, axis=0).T.astype(o_ref.dtype)
            outs.clear()

    for n in range(len(streams) + 2):
        if n + lookahead < len(streams):
            ready.append(prepare(n + lookahead, *streams[n + lookahead]))
        st = ready[n] if n < len(streams) else None
        pr = ready[n - 1] if 0 <= n - 1 < len(streams) else None
        pv = ready[n - 2] if 0 <= n - 2 < len(streams) else None
        cur_steps = order(st.i) if st is not None else []
        prev_steps = order(pr.i) if pr is not None else []
        n_steps = max(len(cur_steps), len(prev_steps))
        for k in range(n_steps):
            if pv is not None and k == n_steps // 2:
                emit_pv(pv)
                pv = None
            if k < len(cur_steps):
                score_step(st, cur_steps[k])
            if k < len(prev_steps):
                prob_step(pr, prev_steps[k])
        if pv is not None:
            emit_pv(pv)
        if st is not None:
            st.m = jnp.max(st.m8, axis=0, keepdims=True)


def _attn_call(q, k, v, kx, slope_rows):
    B, T, _ = q.shape
    L = MOBA_BLOCK
    nb = T // L
    W = HEAD_PAIR * HEAD_DIM
    npairs = N_HEADS // HEAD_PAIR
    seq_spec = pl.BlockSpec((1, T, W), lambda b, g: (b, 0, g))
    return pl.pallas_call(
        _attn_kernel,
        grid=(B, npairs),
        in_specs=[seq_spec, seq_spec, seq_spec, _const_spec(kx.shape),
                  pl.BlockSpec((HEAD_PAIR, N_ALIBI_ROWS, L), lambda b, g: (g, 0, 0)),
                  pl.BlockSpec(memory_space=pltpu.SMEM)],
        out_specs=seq_spec,
        out_shape=jax.ShapeDtypeStruct((B, T, ATT_WIDTH), BF16),
        scratch_shapes=[pltpu.VMEM((nb, L, 2 * W), BF16),
                        pltpu.VMEM((HEAD_PAIR, HEAD_DIM + ONES_ROWS, T), BF16),
                        pltpu.VMEM((nb, W), F32),
                        pltpu.VMEM((ATTN_LOOKAHEAD + 1, 2 * W, L), BF16),
                        pltpu.VMEM((2, nb, L, L), F32),
                        pltpu.VMEM((2, T, L), BF16)],
        compiler_params=pltpu.CompilerParams(
            dimension_semantics=("arbitrary", "arbitrary"), vmem_limit_bytes=VMEM_LIMIT),
        name="moba_attention",
    )(q, k, v, kx, slope_rows, jnp.zeros((1,), jnp.int32))


def _ffn_kernel(x_ref, ylru_ref, yatt_ref, p_ref, wout_ref, lnf_ref, wup_ref, fw_ref, fb_ref, wdown_ref,
                lnp_ref, wg_ref, wp_ref, o_ref, gbuf, gcar):
    t = pl.program_id(1)
    tt = x_ref.shape[1]
    d_ff = wdown_ref.shape[0]
    pad = V7X_SUBLANES

    @pl.when(t == 0)
    def _():
        gcar[...] = jnp.zeros_like(gcar)

    def chain(r):
        rows = slice(r * FFN_SUB_ROWS, (r + 1) * FFN_SUB_ROWS)
        sub = FFN_SUB_ROWS
        h = (x_ref[0, rows, :] + _dot(ylru_ref[0, rows, :], wout_ref[0:LRU_WIDTH, :])
             + _dot(yatt_ref[0, rows, :], wout_ref[LRU_WIDTH:LRU_WIDTH + ATT_WIDTH, :]))
        yield
        u = _rmsnorm(h, lnf_ref[...]).astype(BF16)
        yield
        acc = jnp.zeros_like(h)
        for c in range(d_ff // FF_CHUNK):
            lo = c * FF_CHUNK
            g = _dot(u, wup_ref[:, lo:lo + FF_CHUNK])
            vv = _dot(u, wup_ref[:, d_ff + lo:d_ff + lo + FF_CHUNK])
            gbuf[r, 0:pad, :] = gcar[c]
            gbuf[r, pad:pad + sub, :] = g
            gcar[c] = g[sub - pad:sub, :]
            gc = fb_ref[:, lo:lo + FF_CHUNK] + fw_ref[FFN_CONV - 1:FFN_CONV, lo:lo + FF_CHUNK] * g
            for kk in range(FFN_CONV - 1):
                gc = gc + fw_ref[kk:kk + 1, lo:lo + FF_CHUNK] * gbuf[r, pl.ds(pad - (FFN_CONV - 1) + kk, sub), :]
            act = (jax.nn.gelu(gc) * vv).astype(BF16)
            acc = acc + _dot(act, wdown_ref[lo:lo + FF_CHUNK, :])
            yield
        h = h + acc
        un = _rmsnorm(h, lnp_ref[...]).astype(BF16)
        yield
        gate = jax.nn.sigmoid(_dot(un, wg_ref[...]))
        o_ref[0, rows, :] = h + gate * _dot(p_ref[0, rows, :].astype(BF16), wp_ref[...])
        yield

    _run_skewed([chain(r) for r in range(tt // FFN_SUB_ROWS)], n_phase=4 + d_ff // FF_CHUNK)


def _ffn_call(x, ylru, yatt, p, wout, lnf, wup, fw, fb, wdown, lnp, wg, wp):
    B, T, D = x.shape
    tt = min(ROW_TILE, T)
    d_ff = wdown.shape[0]
    row_spec = lambda w: pl.BlockSpec((1, tt, w), lambda b, t: (b, t, 0))
    consts = (wout, lnf, wup, fw, fb, wdown, lnp, wg, wp)
    return pl.pallas_call(
        _ffn_kernel,
        grid=(B, T // tt),
        in_specs=[row_spec(D), row_spec(LRU_WIDTH), row_spec(ATT_WIDTH), row_spec(p.shape[-1])]
                 + [_const_spec(c.shape) for c in consts],
        out_specs=row_spec(D),
        out_shape=jax.ShapeDtypeStruct((B, T, D), x.dtype),
        scratch_shapes=[pltpu.VMEM((tt // FFN_SUB_ROWS, FFN_SUB_ROWS + V7X_SUBLANES, FF_CHUNK), F32),
                        pltpu.VMEM((d_ff // FF_CHUNK, V7X_SUBLANES, FF_CHUNK), F32)],
        compiler_params=pltpu.CompilerParams(
            dimension_semantics=("arbitrary", "arbitrary"), vmem_limit_bytes=VMEM_LIMIT),
        name="outproj_ffn_ple",
    )(x, ylru, yatt, p, *consts)


def _block_diag(w):
    n, c, d = w.shape
    return jnp.einsum("ncd,nm->ncmd", w, jnp.eye(n, dtype=w.dtype)).reshape(n * c, n * d)


def _gate_weights(wa, wx):
    per = V7X_MXU_DIM // LRU_BLOCK_DIM
    groups = []
    for hf in range(LRU_WIDTH // V7X_MXU_DIM):
        sl = slice(hf * per, (hf + 1) * per)
        groups.append(jnp.concatenate([_block_diag(wa[sl]), _block_diag(wx[sl])], axis=1))
    return jnp.stack(groups).astype(BF16)


def _split3_bf16(s):
    s1 = s.astype(BF16).astype(F32)
    s2 = (s - s1).astype(BF16).astype(F32)
    s3 = (s - s1 - s2).astype(BF16).astype(F32)
    return s1, s2, s3


def _attention_constants(T):
    nb = T // MOBA_BLOCK
    W = HEAD_PAIR * HEAD_DIM
    pos = np.arange(T, dtype=np.int32)
    onehot = (pos[:, None] // MOBA_BLOCK == np.arange(nb)[None, :]).astype(np.float32)
    hi = ((pos >> 6) << 6).astype(np.float32)[:, None]
    lo = (pos & 63).astype(np.float32)[:, None]
    cols = [onehot, hi, hi, hi, lo, lo, lo]
    kx = np.concatenate(cols + [np.zeros((T, W - nb - 6), np.float32)], axis=1)
    slopes = np.float32(2.0) ** (np.float32(-8.0) * np.arange(1, N_HEADS + 1, dtype=np.float32) / np.float32(N_HEADS))
    s1, s2, s3 = _split3_bf16(jnp.asarray(slopes * np.float32(LOG2E)))
    rows = jnp.stack([s1, s2, s3, s1, s2, s3] + [jnp.zeros_like(s1)] * (N_ALIBI_ROWS - 6), axis=1)
    slope_rows = jnp.broadcast_to(rows[:, :, None], (N_HEADS, N_ALIBI_ROWS, MOBA_BLOCK)).astype(F32)
    return jnp.asarray(kx, dtype=BF16), slope_rows


def kernel(x, p, ln_mix, w_in, conv_w, conv_b, gate_a_w, gate_a_b, gate_x_w, gate_x_b, lru_lambda, q_gain,
           k_gain, w_out, ln_ffn, w_up, ffn_conv_w, ffn_conv_b, w_down, ln_ple, w_ple_gate, w_ple_proj):
    B, T, D = x.shape
    depth = w_in.shape[0]
    assert T % MOBA_BLOCK == 0 and T % min(ROW_TILE, T) == 0
    row = lambda a: a.reshape(1, -1)
    seg = jnp.asarray(np.kron(np.eye(V7X_MXU_DIM // HEAD_DIM), np.ones((HEAD_DIM, HEAD_DIM))), dtype=BF16)
    kx, slope_rows = _attention_constants(T)
    h = x
    for i in range(depth):
        ylru, q, k, v = _inproj_call(
            h, row(ln_mix[i]), w_in[i].astype(BF16), conv_w[i], row(conv_b[i]),
            _gate_weights(gate_a_w[i], gate_x_w[i]), row(gate_a_b[i]), row(gate_x_b[i]), row(lru_lambda[i]),
            row(jnp.tile(q_gain[i], N_HEADS)), row(jnp.tile(k_gain[i], N_HEADS)), seg)
        yatt = _attn_call(q, k, v, kx, slope_rows)
        h = _ffn_call(
            h, ylru, yatt, p[i].astype(h.dtype), w_out[i].astype(BF16), row(ln_ffn[i]), w_up[i].astype(BF16),
            ffn_conv_w[i], row(ffn_conv_b[i]), w_down[i].astype(BF16), row(ln_ple[i]),
            w_ple_gate[i].astype(BF16), w_ple_proj[i].astype(BF16))
    return h
```

```python
import functools

import numpy as np
import jax
import jax.numpy as jnp
from jax import lax
from jax.experimental import pallas as pl
from jax.experimental.pallas import tpu as pltpu

F32 = jnp.float32
BF16 = jnp.bfloat16

LRU_WIDTH = 512
LRU_BLOCKS = 8
LRU_BLOCK_DIM = LRU_WIDTH // LRU_BLOCKS
LRU_CONV = 4
LRU_C = 8.0
N_HEADS = 8
HEAD_DIM = 64
ATT_WIDTH = N_HEADS * HEAD_DIM
MOBA_BLOCK = 256
MOBA_TOPK = 3
FFN_CONV = 3
EPS = 1e-6
NEG = -1e30
LOG2E = float(np.log2(np.e))

V7X_LANES = 128
V7X_SUBLANES = 8
V7X_MXU_DIM = 256
V7X_VMEM_BYTES = 64 * 1024 * 1024

ROW_TILE = 512
HEAD_PAIR = V7X_LANES // HEAD_DIM
INPROJ_SUB_ROWS = 256
FF_CHUNK = 1536
FFN_SUB_ROWS = 256
N_ALIBI_ROWS = V7X_SUBLANES
ONES_ROWS = 16
ATTN_LOOKAHEAD = 2
VMEM_LIMIT = 56 * 1024 * 1024


def _const_spec(shape):
    nd = len(shape)
    return pl.BlockSpec(shape, lambda *_: (0,) * nd, pipeline_mode=pl.Buffered(1))


def _rmsnorm(x, g):
    return x * lax.rsqrt(jnp.mean(x * x, axis=-1, keepdims=True) + EPS) * g


def _dot(a, b):
    return jnp.dot(a, b, preferred_element_type=F32)


def _run_skewed(chains, n_phase):
    for step in range(n_phase + len(chains) - 1):
        for r, ch in enumerate(chains):
            if 0 <= step - r < n_phase:
                next(ch)


def _segment_sumsq(z, seg_ref):
    sq = z * z
    parts = []
    for hf in range(z.shape[1] // V7X_MXU_DIM):
        s = sq[:, hf * V7X_MXU_DIM:(hf + 1) * V7X_MXU_DIM]
        hi = s.astype(BF16)
        lo = (s - hi.astype(F32)).astype(BF16)
        parts.append(_dot(hi, seg_ref[...]) + _dot(lo, seg_ref[...]))
    return jnp.concatenate(parts, axis=1)


def _linear_scan(a, b, h_in):
    n, c = a.shape
    sl = V7X_SUBLANES
    a3 = a.reshape(n // sl, sl, c)
    b3 = b.reshape(n // sl, sl, c)
    sub = lax.broadcasted_iota(jnp.int32, a3.shape, 1)
    s = 1
    while s < sl:
        keep = sub >= s
        a_sh = jnp.where(keep, pltpu.roll(a3, s, 1), 1.0)
        b_sh = jnp.where(keep, pltpu.roll(b3, s, 1), 0.0)
        b3 = b3 + a3 * b_sh
        a3 = a3 * a_sh
        s *= 2
    carry = h_in
    groups = []
    for g in range(n // sl):
        hg = b3[g] + a3[g] * carry
        groups.append(hg)
        carry = hg[sl - 1:sl, :]
    return jnp.concatenate(groups, axis=0)


def _inproj_kernel(x_ref, ln_ref, win_ref, cw_ref, cb_ref, wg_ref, ba_ref, bx_ref, lam_ref,
                   qg_ref, kg_ref, seg_ref,
                   ylru_ref, q_ref, k_ref, v_ref,
                   xbuf, hcarry):
    t = pl.program_id(1)
    tt = x_ref.shape[1]

    @pl.when(t == 0)
    def _():
        xbuf[0:V7X_SUBLANES, :] = jnp.zeros((V7X_SUBLANES, LRU_WIDTH), F32)
        hcarry[...] = jnp.zeros_like(hcarry)

    nl = -lam_ref[...]
    softplus = jnp.maximum(nl, 0.0) + jnp.log(1.0 + jnp.exp(-jnp.abs(nl)))
    sub = INPROJ_SUB_ROWS
    n_groups = win_ref.shape[1] // LRU_WIDTH
    carry = [hcarry[...]]

    def chain(r):
        rows = slice(r * sub, (r + 1) * sub)
        box = {}

        def project():
            u = _rmsnorm(x_ref[0, rows, :], ln_ref[...]).astype(BF16)
            yield
            parts = []
            for g in range(n_groups):
                parts.append(_dot(u, win_ref[:, g * LRU_WIDTH:(g + 1) * LRU_WIDTH]))
                yield
            box["parts"] = parts

        def tail():
            x_lru, g_lru, q, k, v = box["parts"]
            v_ref[0, rows, :] = v.astype(v_ref.dtype)
            base = V7X_SUBLANES + r * sub
            xbuf[base:base + sub, :] = x_lru
            xc = cb_ref[...] + cw_ref[LRU_CONV - 1:LRU_CONV, :] * x_lru
            for kk in range(LRU_CONV - 1):
                xc = xc + cw_ref[kk:kk + 1, :] * xbuf[pl.ds(base - (LRU_CONV - 1) + kk, sub), :]
            yield
            xcb = xc.astype(BF16)
            r_parts, i_parts = [], []
            for hf in range(LRU_WIDTH // V7X_MXU_DIM):
                z = _dot(xcb[:, hf * V7X_MXU_DIM:(hf + 1) * V7X_MXU_DIM], wg_ref[hf])
                r_parts.append(z[:, 0:V7X_MXU_DIM])
                i_parts.append(z[:, V7X_MXU_DIM:2 * V7X_MXU_DIM])
            rg = jax.nn.sigmoid(jnp.concatenate(r_parts, axis=1) + ba_ref[...])
            ig = jax.nn.sigmoid(jnp.concatenate(i_parts, axis=1) + bx_ref[...])
            a = jnp.exp((-LRU_C) * rg * softplus)
            z = 1.0 - a * a
            mult = jnp.where(z > 0.0, z * lax.rsqrt(z), 0.0)
            gx = ig * xc
            b = mult * gx
            if r == 0:
                first = lax.broadcasted_iota(jnp.int32, (V7X_SUBLANES, LRU_WIDTH), 0) + t * tt == 0
                b = jnp.concatenate([jnp.where(first, gx[0:V7X_SUBLANES, :], b[0:V7X_SUBLANES, :]),
                                     b[V7X_SUBLANES:, :]], axis=0)
            yield
            h = _linear_scan(a, b, carry[0])
            carry[0] = h[sub - 1:sub, :]
            ylru_ref[0, rows, :] = (h * jax.nn.gelu(g_lru)).astype(ylru_ref.dtype)
            yield
            qn = q * lax.rsqrt(_segment_sumsq(q, seg_ref) * (1.0 / HEAD_DIM) + EPS) * qg_ref[...]
            q_ref[0, rows, :] = qn.astype(q_ref.dtype)
            yield
            kn = k * lax.rsqrt(_segment_sumsq(k, seg_ref) * (1.0 / HEAD_DIM) + EPS) * kg_ref[...]
            k_ref[0, rows, :] = kn.astype(k_ref.dtype)
            yield

        return project(), tail()

    def drain(gen):
        for _ in gen:
            pass

    pairs = [chain(r) for r in range(tt // sub)]
    drain(pairs[0][0])
    for r in range(1, len(pairs)):
        proj, prev_tail = pairs[r][0], pairs[r - 1][1]
        next(proj)
        while True:
            proj_done = next(proj, "done") == "done"
            tail_done = next(prev_tail, "done") == "done"
            if proj_done and tail_done:
                break
    drain(pairs[-1][1])
    xbuf[0:V7X_SUBLANES, :] = xbuf[tt:tt + V7X_SUBLANES, :]
    hcarry[...] = carry[0]


def _inproj_call(x, ln, win, cw, cb, wg, ba, bx, lam, qg, kg, seg):
    B, T, D = x.shape
    tt = min(ROW_TILE, T)
    row_spec = lambda w: pl.BlockSpec((1, tt, w), lambda b, t: (b, t, 0))
    consts = (ln, win, cw, cb, wg, ba, bx, lam, qg, kg, seg)
    return pl.pallas_call(
        _inproj_kernel,
        grid=(B, T // tt),
        in_specs=[row_spec(D)] + [_const_spec(c.shape) for c in consts],
        out_specs=[row_spec(LRU_WIDTH), row_spec(ATT_WIDTH), row_spec(ATT_WIDTH), row_spec(ATT_WIDTH)],
        out_shape=[jax.ShapeDtypeStruct((B, T, LRU_WIDTH), BF16),
                   jax.ShapeDtypeStruct((B, T, ATT_WIDTH), F32),
                   jax.ShapeDtypeStruct((B, T, ATT_WIDTH), F32),
                   jax.ShapeDtypeStruct((B, T, ATT_WIDTH), BF16)],
        scratch_shapes=[pltpu.VMEM((tt + V7X_SUBLANES, LRU_WIDTH), F32),
                        pltpu.VMEM((1, LRU_WIDTH), F32)],
        compiler_params=pltpu.CompilerParams(
            dimension_semantics=("arbitrary", "arbitrary"), vmem_limit_bytes=VMEM_LIMIT),
        name="inproj_rglru",
    )(x, *consts)


def _select_bias(gate, i):
    nb = gate.shape[0]
    rowi = lax.broadcasted_iota(jnp.int32, gate.shape, 0)
    rowf = rowi.astype(F32)
    ninf = -jnp.inf
    g = jnp.where(rowi < i, gate, ninf)
    sel = rowi == i
    for _ in range(min(MOBA_TOPK, i)):
        mx = jnp.max(g, axis=0, keepdims=True)
        cand = jnp.logical_and(g == mx, g > ninf)
        first = jnp.min(jnp.where(cand, rowf, float(nb)), axis=0, keepdims=True)
        pick = jnp.logical_and(rowf == first, cand)
        sel = jnp.logical_or(sel, pick)
        g = jnp.where(pick, ninf, g)
    return jnp.where(sel, 0.0, NEG)


def _attn_kernel(q_ref, k_ref, v_ref, kx_ref, slope_ref, zero_ref, o_ref,
                 kp_ref, vt_ref, km_ref, qp_ref, s_ref, p_ref):
    L = MOBA_BLOCK
    nb = k_ref.shape[1] // L
    W = HEAD_PAIR * HEAD_DIM
    rows8 = (L // V7X_SUBLANES, V7X_SUBLANES, L)

    for j in range(nb):
        kj = k_ref[0, j * L:(j + 1) * L, :]
        kp_ref[j, :, 0:W] = kj.astype(BF16)
        kp_ref[j, :, W:2 * W] = kx_ref[j * L:(j + 1) * L, :]
        km_ref[j:j + 1, :] = jnp.mean(kj, axis=0, keepdims=True)
        vt = v_ref[0, j * L:(j + 1) * L, :].astype(F32).T.astype(BF16)
        for hh in range(HEAD_PAIR):
            vt_ref[hh, 0:HEAD_DIM, j * L:(j + 1) * L] = vt[hh * HEAD_DIM:(hh + 1) * HEAD_DIM, :]
    for hh in range(HEAD_PAIR):
        vt_ref[hh, HEAD_DIM:HEAD_DIM + ONES_ROWS, :] = jnp.ones((ONES_ROWS, nb * L), BF16)

    lane = lax.broadcasted_iota(jnp.int32, (L, W), 1)
    causal = lax.broadcasted_iota(jnp.int32, (L, L), 0) <= lax.broadcasted_iota(jnp.int32, (L, L), 1)
    pad_rows = jnp.zeros((W - nb - N_ALIBI_ROWS, L), F32)
    qscale = LOG2E / np.sqrt(HEAD_DIM)

    dyn0 = zero_ref[0]

    class Stream:
        pass

    def prepare(n, i, hh):
        st = Stream()
        st.i, st.hh, st.slot, st.qslot = i, hh, n % 2, n % qp_ref.shape[0]
        qf = q_ref[0, i * L:(i + 1) * L, :]
        qm = jnp.where(jnp.logical_and(lane >= hh * HEAD_DIM, lane < (hh + 1) * HEAD_DIM), qf, 0.0)
        gate = lax.dot_general(km_ref[...], qm, (((1,), (1,)), ((), ())),
                               precision=lax.Precision.HIGHEST, preferred_element_type=F32)
        qp_ref[st.qslot] = jnp.concatenate([(qm * qscale).T, _select_bias(gate, i), slope_ref[hh], pad_rows],
                                           axis=0).astype(BF16)
        st.m8 = None
        return st

    def score_step(st, c):
        s = _dot(kp_ref[c], qp_ref[st.qslot])
        if c == st.i:
            s = jnp.where(causal, s, NEG)
        s_ref[st.slot + dyn0, c] = s
        m8 = jnp.max(s.reshape(rows8), axis=0)
        st.m8 = m8 if st.m8 is None else jnp.maximum(st.m8, m8)

    def prob_step(st, c):
        p_ref[st.slot, c * L:(c + 1) * L, :] = jnp.exp2(s_ref[st.slot + dyn0, c] - st.m).astype(BF16)

    def finish(st):
        k_len = (st.i + 1) * L
        acc = _dot(vt_ref[st.hh, :, 0:k_len], p_ref[st.slot, 0:k_len, :])
        return acc[0:HEAD_DIM, :] / acc[HEAD_DIM:HEAD_DIM + 1, :]

    def order(i):
        return [i] + list(range(i))

    streams = [(i, hh) for i in range(nb) for hh in range(HEAD_PAIR)]
    outs = []
    lookahead = qp_ref.shape[0] - 1
    ready = [prepare(n, *streams[n]) for n in range(min(lookahead, len(streams)))]

    def emit_pv(st):
        outs.append(finish(st))
        if st.hh == HEAD_PAIR - 1:
            o_ref[0, st.i * L:(st.i + 1) * L, :] = jnp.concatenate(outs, axis=0).T.astype(o_ref.dtype)
            outs.clear()

    for n in range(len(streams) + 2):
        if n + lookahead < len(streams):
            ready.append(prepare(n + lookahead, *streams[n + lookahead]))
        st = ready[n] if n < len(streams) else None
        pr = ready[n - 1] if 0 <= n - 1 < len(streams) else None
        pv = ready[n - 2] if 0 <= n - 2 < len(streams) else None
        cur_steps = order(st.i) if st is not None else []
        prev_steps = order(pr.i) if pr is not None else []
        n_steps = max(len(cur_steps), len(prev_steps))
        for k in range(n_steps):
            if pv is not None and k == n_steps // 2:
                emit_pv(pv)
                pv = None
            if k < len(cur_steps):
                score_step(st, cur_steps[k])
            if k < len(prev_steps):
                prob_step(pr, prev_steps[k])
        if pv is not None:
            emit_pv(pv)
        if st is not None:
            st.m = jnp.max(st.m8, axis=0, keepdims=True)


def _attn_call(q, k, v, kx, slope_rows):
    B, T, _ = q.shape
    L = MOBA_BLOCK
    nb = T // L
    W = HEAD_PAIR * HEAD_DIM
    npairs = N_HEADS // HEAD_PAIR
    seq_spec = pl.BlockSpec((1, T, W), lambda b, g: (b, 0, g))
    return pl.pallas_call(
        _attn_kernel,
        grid=(B, npairs),
        in_specs=[seq_spec, seq_spec, seq_spec, _const_spec(kx.shape),
                  pl.BlockSpec((HEAD_PAIR, N_ALIBI_ROWS, L), lambda b, g: (g, 0, 0)),
                  pl.BlockSpec(memory_space=pltpu.SMEM)],
        out_specs=seq_spec,
        out_shape=jax.ShapeDtypeStruct((B, T, ATT_WIDTH), BF16),
        scratch_shapes=[pltpu.VMEM((nb, L, 2 * W), BF16),
                        pltpu.VMEM((HEAD_PAIR, HEAD_DIM + ONES_ROWS, T), BF16),
                        pltpu.VMEM((nb, W), F32),
                        pltpu.VMEM((ATTN_LOOKAHEAD + 1, 2 * W, L), BF16),
                        pltpu.VMEM((2, nb, L, L), F32),
                        pltpu.VMEM((2, T, L), BF16)],
        compiler_params=pltpu.CompilerParams(
            dimension_semantics=("arbitrary", "arbitrary"), vmem_limit_bytes=VMEM_LIMIT),
        name="moba_attention",
    )(q, k, v, kx, slope_rows, jnp.zeros((1,), jnp.int32))


def _ffn_kernel(x_ref, ylru_ref, yatt_ref, p_ref, wout_ref, lnf_ref, wup_ref, fw_ref, fb_ref, wdown_ref,
                lnp_ref, wg_ref, wp_ref, o_ref, gbuf, gcar):
    t = pl.program_id(1)
    tt = x_ref.shape[1]
    d_ff = wdown_ref.shape[0]
    pad = V7X_SUBLANES

    @pl.when(t == 0)
    def _():
        gcar[...] = jnp.zeros_like(gcar)

    def chain(r):
        rows = slice(r * FFN_SUB_ROWS, (r + 1) * FFN_SUB_ROWS)
        sub = FFN_SUB_ROWS
        h = (x_ref[0, rows, :] + _dot(ylru_ref[0, rows, :], wout_ref[0:LRU_WIDTH, :])
             + _dot(yatt_ref[0, rows, :], wout_ref[LRU_WIDTH:LRU_WIDTH + ATT_WIDTH, :]))
        yield
        u = _rmsnorm(h, lnf_ref[...]).astype(BF16)
        yield
        acc = jnp.zeros_like(h)
        for c in range(d_ff // FF_CHUNK):
            lo = c * FF_CHUNK
            g = _dot(u, wup_ref[:, lo:lo + FF_CHUNK])
            vv = _dot(u, wup_ref[:, d_ff + lo:d_ff + lo + FF_CHUNK])
            gbuf[r, 0:pad, :] = gcar[c]
            gbuf[r, pad:pad + sub, :] = g
            gcar[c] = g[sub - pad:sub, :]
            gc = fb_ref[:, lo:lo + FF_CHUNK] + fw_ref[FFN_CONV - 1:FFN_CONV, lo:lo + FF_CHUNK] * g
            for kk in range(FFN_CONV - 1):
                gc = gc + fw_ref[kk:kk + 1, lo:lo + FF_CHUNK] * gbuf[r, pl.ds(pad - (FFN_CONV - 1) + kk, sub), :]
            act = (jax.nn.gelu(gc) * vv).astype(BF16)
            acc = acc + _dot(act, wdown_ref[lo:lo + FF_CHUNK, :])
            yield
        h = h + acc
        un = _rmsnorm(h, lnp_ref[...]).astype(BF16)
        yield
        gate = jax.nn.sigmoid(_dot(un, wg_ref[...]))
        o_ref[0, rows, :] = h + gate * _dot(p_ref[0, rows, :].astype(BF16), wp_ref[...])
        yield

    _run_skewed([chain(r) for r in range(tt // FFN_SUB_ROWS)], n_phase=4 + d_ff // FF_CHUNK)


def _ffn_call(x, ylru, yatt, p, wout, lnf, wup, fw, fb, wdown, lnp, wg, wp):
    B, T, D = x.shape
    tt = min(ROW_TILE, T)
    d_ff = wdown.shape[0]
    row_spec = lambda w: pl.BlockSpec((1, tt, w), lambda b, t: (b, t, 0))
    consts = (wout, lnf, wup, fw, fb, wdown, lnp, wg, wp)
    return pl.pallas_call(
        _ffn_kernel,
        grid=(B, T // tt),
        in_specs=[row_spec(D), row_spec(LRU_WIDTH), row_spec(ATT_WIDTH), row_spec(p.shape[-1])]
                 + [_const_spec(c.shape) for c in consts],
        out_specs=row_spec(D),
        out_shape=jax.ShapeDtypeStruct((B, T, D), x.dtype),
        scratch_shapes=[pltpu.VMEM((tt // FFN_SUB_ROWS, FFN_SUB_ROWS + V7X_SUBLANES, FF_CHUNK), F32),
                        pltpu.VMEM((d_ff // FF_CHUNK, V7X_SUBLANES, FF_CHUNK), F32)],
        compiler_params=pltpu.CompilerParams(
            dimension_semantics=("arbitrary", "arbitrary"), vmem_limit_bytes=VMEM_LIMIT),
        name="outproj_ffn_ple",
    )(x, ylru, yatt, p, *consts)


def _block_diag(w):
    n, c, d = w.shape
    return jnp.einsum("ncd,nm->ncmd", w, jnp.eye(n, dtype=w.dtype)).reshape(n * c, n * d)


def _gate_weights(wa, wx):
    per = V7X_MXU_DIM // LRU_BLOCK_DIM
    groups = []
    for hf in range(LRU_WIDTH // V7X_MXU_DIM):
        sl = slice(hf * per, (hf + 1) * per)
        groups.append(jnp.concatenate([_block_diag(wa[sl]), _block_diag(wx[sl])], axis=1))
    return jnp.stack(groups).astype(BF16)


def _split3_bf16(s):
    s1 = s.astype(BF16).astype(F32)
    s2 = (s - s1).astype(BF16).astype(F32)
    s3 = (s - s1 - s2).astype(BF16).astype(F32)
    return s1, s2, s3


def _attention_constants(T):
    nb = T // MOBA_BLOCK
    W = HEAD_PAIR * HEAD_DIM
    pos = np.arange(T, dtype=np.int32)
    onehot = (pos[:, None] // MOBA_BLOCK == np.arange(nb)[None, :]).astype(np.float32)
    hi = ((pos >> 6) << 6).astype(np.float32)[:, None]
    lo = (pos & 63).astype(np.float32)[:, None]
    cols = [onehot, hi, hi, hi, lo, lo, lo]
    kx = np.concatenate(cols + [np.zeros((T, W - nb - 6), np.float32)], axis=1)
    slopes = np.float32(2.0) ** (np.float32(-8.0) * np.arange(1, N_HEADS + 1, dtype=np.float32) / np.float32(N_HEADS))
    s1, s2, s3 = _split3_bf16(jnp.asarray(slopes * np.float32(LOG2E)))
    rows = jnp.stack([s1, s2, s3, s1, s2, s3] + [jnp.zeros_like(s1)] * (N_ALIBI_ROWS - 6), axis=1)
    slope_rows = jnp.broadcast_to(rows[:, :, None], (N_HEADS, N_ALIBI_ROWS, MOBA_BLOCK)).astype(F32)
    return jnp.asarray(kx, dtype=BF16), slope_rows


def kernel(x, p, ln_mix, w_in, conv_w, conv_b, gate_a_w, gate_a_b, gate_x_w, gate_x_b, lru_lambda, q_gain,
           k_gain, w_out, ln_ffn, w_up, ffn_conv_w, ffn_conv_b, w_down, ln_ple, w_ple_gate, w_ple_proj):
    B, T, D = x.shape
    depth = w_in.shape[0]
    assert T % MOBA_BLOCK == 0 and T % min(ROW_TILE, T) == 0
    row = lambda a: a.reshape(1, -1)
    seg = jnp.asarray(np.kron(np.eye(V7X_MXU_DIM // HEAD_DIM), np.ones((HEAD_DIM, HEAD_DIM))), dtype=BF16)
    kx, slope_rows = _attention_constants(T)
    h = x
    for i in range(depth):
        ylru, q, k, v = _inproj_call(
            h, row(ln_mix[i]), w_in[i].astype(BF16), conv_w[i], row(conv_b[i]),
            _gate_weights(gate_a_w[i], gate_x_w[i]), row(gate_a_b[i]), row(gate_x_b[i]), row(lru_lambda[i]),
            row(jnp.tile(q_gain[i], N_HEADS)), row(jnp.tile(k_gain[i], N_HEADS)), seg)
        yatt = _attn_call(q, k, v, kx, slope_rows)
        h = _ffn_call(
            h, ylru, yatt, p[i].astype(h.dtype), w_out[i].astype(BF16), row(ln_ffn[i]), w_up[i].astype(BF16),
            ffn_conv_w[i], row(ffn_conv_b[i]), w_down[i].astype(BF16), row(ln_ple[i]),
            w_ple_gate[i].astype(BF16), w_ple_proj[i].astype(BF16))
    return h
```

```python
import functools

import numpy as np
import jax
import jax.numpy as jnp
from jax import lax
from jax.experimental import pallas as pl
from jax.experimental.pallas import tpu as pltpu

F32 = jnp.float32
BF16 = jnp.bfloat16

LRU_WIDTH = 512
LRU_BLOCKS = 8
LRU_BLOCK_DIM = LRU_WIDTH // LRU_BLOCKS
LRU_CONV = 4
LRU_C = 8.0
N_HEADS = 8
HEAD_DIM = 64
ATT_WIDTH = N_HEADS * HEAD_DIM
MOBA_BLOCK = 256
MOBA_TOPK = 3
FFN_CONV = 3
EPS = 1e-6
NEG = -1e30
LOG2E = float(np.log2(np.e))

V7X_LANES = 128
V7X_SUBLANES = 8
V7X_MXU_DIM = 256
V7X_VMEM_BYTES = 64 * 1024 * 1024

ROW_TILE = 512
HEAD_PAIR = V7X_LANES // HEAD_DIM
FF_CHUNK = 1536
FFN_SUB_ROWS = 256
N_ALIBI_ROWS = V7X_SUBLANES
ONES_ROWS = 16
ATTN_LOOKAHEAD = 2
VMEM_LIMIT = 56 * 1024 * 1024


def _const_spec(shape):
    nd = len(shape)
    return pl.BlockSpec(shape, lambda *_: (0,) * nd, pipeline_mode=pl.Buffered(1))


def _rmsnorm(x, g):
    return x * lax.rsqrt(jnp.mean(x * x, axis=-1, keepdims=True) + EPS) * g


def _dot(a, b):
    return jnp.dot(a, b, preferred_element_type=F32)


def _run_skewed(chains, n_phase):
    for step in range(n_phase + len(chains) - 1):
        for r, ch in enumerate(chains):
            if 0 <= step - r < n_phase:
                next(ch)


def _segment_sumsq(z, seg_ref):
    sq = z * z
    parts = []
    for hf in range(z.shape[1] // V7X_MXU_DIM):
        s = sq[:, hf * V7X_MXU_DIM:(hf + 1) * V7X_MXU_DIM]
        hi = s.astype(BF16)
        lo = (s - hi.astype(F32)).astype(BF16)
        parts.append(_dot(hi, seg_ref[...]) + _dot(lo, seg_ref[...]))
    return jnp.concatenate(parts, axis=1)


def _linear_scan(a, b, h_in):
    n, c = a.shape
    sl = V7X_SUBLANES
    a3 = a.reshape(n // sl, sl, c)
    b3 = b.reshape(n // sl, sl, c)
    sub = lax.broadcasted_iota(jnp.int32, a3.shape, 1)
    s = 1
    while s < sl:
        keep = sub >= s
        a_sh = jnp.where(keep, pltpu.roll(a3, s, 1), 1.0)
        b_sh = jnp.where(keep, pltpu.roll(b3, s, 1), 0.0)
        b3 = b3 + a3 * b_sh
        a3 = a3 * a_sh
        s *= 2
    carry = h_in
    groups = []
    for g in range(n // sl):
        hg = b3[g] + a3[g] * carry
        groups.append(hg)
        carry = hg[sl - 1:sl, :]
    return jnp.concatenate(groups, axis=0)


def _inproj_kernel(x_ref, ln_ref, win_ref, cw_ref, cb_ref, wg_ref, ba_ref, bx_ref, lam_ref,
                   qg_ref, kg_ref, seg_ref,
                   ylru_ref, q_ref, k_ref, v_ref,
                   xbuf, hcarry):
    t = pl.program_id(1)
    tt = x_ref.shape[1]

    @pl.when(t == 0)
    def _():
        xbuf[0:V7X_SUBLANES, :] = jnp.zeros((V7X_SUBLANES, LRU_WIDTH), F32)
        hcarry[...] = jnp.zeros_like(hcarry)

    u = _rmsnorm(x_ref[0], ln_ref[...])
    proj = _dot(u.astype(BF16), win_ref[...])
    x_lru = proj[:, 0:LRU_WIDTH]
    g_lru = proj[:, LRU_WIDTH:2 * LRU_WIDTH]
    o = 2 * LRU_WIDTH
    q = proj[:, o:o + ATT_WIDTH]
    k = proj[:, o + ATT_WIDTH:o + 2 * ATT_WIDTH]
    v = proj[:, o + 2 * ATT_WIDTH:o + 3 * ATT_WIDTH]

    xbuf[V7X_SUBLANES:V7X_SUBLANES + tt, :] = x_lru
    xc = cb_ref[...] + cw_ref[LRU_CONV - 1:LRU_CONV, :] * x_lru
    for kk in range(LRU_CONV - 1):
        start = V7X_SUBLANES - (LRU_CONV - 1) + kk
        xc = xc + cw_ref[kk:kk + 1, :] * xbuf[pl.ds(start, tt), :]
    xbuf[0:V7X_SUBLANES, :] = x_lru[tt - V7X_SUBLANES:tt, :]

    xcb = xc.astype(BF16)
    r_parts, i_parts = [], []
    for hf in range(LRU_WIDTH // V7X_MXU_DIM):
        z = _dot(xcb[:, hf * V7X_MXU_DIM:(hf + 1) * V7X_MXU_DIM], wg_ref[hf])
        r_parts.append(z[:, 0:V7X_MXU_DIM])
        i_parts.append(z[:, V7X_MXU_DIM:2 * V7X_MXU_DIM])
    r = jax.nn.sigmoid(jnp.concatenate(r_parts, axis=1) + ba_ref[...])
    ig = jax.nn.sigmoid(jnp.concatenate(i_parts, axis=1) + bx_ref[...])

    nl = -lam_ref[...]
    softplus = jnp.maximum(nl, 0.0) + jnp.log(1.0 + jnp.exp(-jnp.abs(nl)))
    log_a = (-LRU_C) * r * softplus
    a = jnp.exp(log_a)
    z = 1.0 - a * a
    mult = jnp.where(z > 0.0, z * lax.rsqrt(z), 0.0)
    gx = ig * xc
    b = mult * gx
    first = lax.broadcasted_iota(jnp.int32, (V7X_SUBLANES, LRU_WIDTH), 0) + t * tt == 0
    b = jnp.concatenate([jnp.where(first, gx[0:V7X_SUBLANES, :], b[0:V7X_SUBLANES, :]), b[V7X_SUBLANES:, :]], axis=0)

    h = _linear_scan(a, b, hcarry[...])
    hcarry[...] = h[tt - 1:tt, :]
    ylru_ref[0] = (h * jax.nn.gelu(g_lru)).astype(ylru_ref.dtype)

    qn = q * lax.rsqrt(_segment_sumsq(q, seg_ref) * (1.0 / HEAD_DIM) + EPS) * qg_ref[...]
    kn = k * lax.rsqrt(_segment_sumsq(k, seg_ref) * (1.0 / HEAD_DIM) + EPS) * kg_ref[...]
    q_ref[0] = qn.astype(q_ref.dtype)
    k_ref[0] = kn.astype(k_ref.dtype)
    v_ref[0] = v.astype(v_ref.dtype)


def _inproj_call(x, ln, win, cw, cb, wg, ba, bx, lam, qg, kg, seg):
    B, T, D = x.shape
    tt = min(ROW_TILE, T)
    row_spec = lambda w: pl.BlockSpec((1, tt, w), lambda b, t: (b, t, 0))
    consts = (ln, win, cw, cb, wg, ba, bx, lam, qg, kg, seg)
    return pl.pallas_call(
        _inproj_kernel,
        grid=(B, T // tt),
        in_specs=[row_spec(D)] + [_const_spec(c.shape) for c in consts],
        out_specs=[row_spec(LRU_WIDTH), row_spec(ATT_WIDTH), row_spec(ATT_WIDTH), row_spec(ATT_WIDTH)],
        out_shape=[jax.ShapeDtypeStruct((B, T, LRU_WIDTH), BF16),
                   jax.ShapeDtypeStruct((B, T, ATT_WIDTH), F32),
                   jax.ShapeDtypeStruct((B, T, ATT_WIDTH), F32),
                   jax.ShapeDtypeStruct((B, T, ATT_WIDTH), BF16)],
        scratch_shapes=[pltpu.VMEM((tt + V7X_SUBLANES, LRU_WIDTH), F32),
                        pltpu.VMEM((1, LRU_WIDTH), F32)],
        compiler_params=pltpu.CompilerParams(
            dimension_semantics=("arbitrary", "arbitrary"), vmem_limit_bytes=VMEM_LIMIT),
        name="inproj_rglru",
    )(x, *consts)


def _select_bias(gate, i):
    nb = gate.shape[0]
    rowi = lax.broadcasted_iota(jnp.int32, gate.shape, 0)
    rowf = rowi.astype(F32)
    ninf = -jnp.inf
    g = jnp.where(rowi < i, gate, ninf)
    sel = rowi == i
    for _ in range(min(MOBA_TOPK, i)):
        mx = jnp.max(g, axis=0, keepdims=True)
        cand = jnp.logical_and(g == mx, g > ninf)
        first = jnp.min(jnp.where(cand, rowf, float(nb)), axis=0, keepdims=True)
        pick = jnp.logical_and(rowf == first, cand)
        sel = jnp.logical_or(sel, pick)
        g = jnp.where(pick, ninf, g)
    return jnp.where(sel, 0.0, NEG)


def _attn_kernel(q_ref, k_ref, v_ref, kx_ref, slope_ref, zero_ref, o_ref,
                 kp_ref, vt_ref, km_ref, qp_ref, s_ref, p_ref):
    L = MOBA_BLOCK
    nb = k_ref.shape[1] // L
    W = HEAD_PAIR * HEAD_DIM
    rows8 = (L // V7X_SUBLANES, V7X_SUBLANES, L)

    for j in range(nb):
        kj = k_ref[0, j * L:(j + 1) * L, :]
        kp_ref[j, :, 0:W] = kj.astype(BF16)
        kp_ref[j, :, W:2 * W] = kx_ref[j * L:(j + 1) * L, :]
        km_ref[j:j + 1, :] = jnp.mean(kj, axis=0, keepdims=True)
        vt = v_ref[0, j * L:(j + 1) * L, :].astype(F32).T.astype(BF16)
        for hh in range(HEAD_PAIR):
            vt_ref[hh, 0:HEAD_DIM, j * L:(j + 1) * L] = vt[hh * HEAD_DIM:(hh + 1) * HEAD_DIM, :]
    for hh in range(HEAD_PAIR):
        vt_ref[hh, HEAD_DIM:HEAD_DIM + ONES_ROWS, :] = jnp.ones((ONES_ROWS, nb * L), BF16)

    lane = lax.broadcasted_iota(jnp.int32, (L, W), 1)
    causal = lax.broadcasted_iota(jnp.int32, (L, L), 0) <= lax.broadcasted_iota(jnp.int32, (L, L), 1)
    pad_rows = jnp.zeros((W - nb - N_ALIBI_ROWS, L), F32)
    qscale = LOG2E / np.sqrt(HEAD_DIM)

    dyn0 = zero_ref[0]

    class Stream:
        pass

    def prepare(n, i, hh):
        st = Stream()
        st.i, st.hh, st.slot, st.qslot = i, hh, n % 2, n % qp_ref.shape[0]
        qf = q_ref[0, i * L:(i + 1) * L, :]
        qm = jnp.where(jnp.logical_and(lane >= hh * HEAD_DIM, lane < (hh + 1) * HEAD_DIM), qf, 0.0)
        gate = lax.dot_general(km_ref[...], qm, (((1,), (1,)), ((), ())),
                               precision=lax.Precision.HIGHEST, preferred_element_type=F32)
        qp_ref[st.qslot] = jnp.concatenate([(qm * qscale).T, _select_bias(gate, i), slope_ref[hh], pad_rows],
                                           axis=0).astype(BF16)
        st.m8 = None
        return st

    def score_step(st, c):
        s = _dot(kp_ref[c], qp_ref[st.qslot])
        if c == st.i:
            s = jnp.where(causal, s, NEG)
        s_ref[st.slot + dyn0, c] = s
        m8 = jnp.max(s.reshape(rows8), axis=0)
        st.m8 = m8 if st.m8 is None else jnp.maximum(st.m8, m8)

    def prob_step(st, c):
        p_ref[st.slot, c * L:(c + 1) * L, :] = jnp.exp2(s_ref[st.slot + dyn0, c] - st.m).astype(BF16)

    def finish(st):
        k_len = (st.i + 1) * L
        acc = _dot(vt_ref[st.hh, :, 0:k_len], p_ref[st.slot, 0:k_len, :])
        return acc[0:HEAD_DIM, :] / acc[HEAD_DIM:HEAD_DIM + 1, :]

    def order(i):
        return [i] + list(range(i))

    streams = [(i, hh) for i in range(nb) for hh in range(HEAD_PAIR)]
    outs = []
    lookahead = qp_ref.shape[0] - 1
    ready = [prepare(n, *streams[n]) for n in range(min(lookahead, len(streams)))]

    def emit_pv(st):
        outs.append(finish(st))
        if st.hh == HEAD_PAIR - 1:
            o_ref[0, st.i * L:(st.i + 1) * L, :] = jnp.concatenate(outs, axis=0).T.astype(o_ref.dtype)
            outs.clear()

    for n in range(len(streams) + 2):
        if n + lookahead < len(streams):
            ready.append(prepare(n + lookahead, *streams[n + lookahead]))
        st = ready[n] if n < len(streams) else None
        pr = ready[n - 1] if 0 <= n - 1 < len(streams) else None
        pv = ready[n - 2] if 0 <= n - 2 < len(streams) else None
        cur_steps = order(st.i) if st is not None else []
        prev_steps = order(pr.i) if pr is not None else []
        n_steps = max(len(cur_steps), len(prev_steps))
        for k in range(n_steps):
            if pv is not None and k == n_steps // 2:
                emit_pv(pv)
                pv = None
            if k < len(cur_steps):
                score_step(st, cur_steps[k])
            if k < len(prev_steps):
                prob_step(pr, prev_steps[k])
        if pv is not None:
            emit_pv(pv)
        if st is not None:
            st.m = jnp.max(st.m8, axis=0, keepdims=True)


def _attn_call(q, k, v, kx, slope_rows):
    B, T, _ = q.shape
    L = MOBA_BLOCK
    nb = T // L
    W = HEAD_PAIR * HEAD_DIM
    npairs = N_HEADS // HEAD_PAIR
    seq_spec = pl.BlockSpec((1, T, W), lambda b, g: (b, 0, g))
    return pl.pallas_call(
        _attn_kernel,
        grid=(B, npairs),
        in_specs=[seq_spec, seq_spec, seq_spec, _const_spec(kx.shape),
                  pl.BlockSpec((HEAD_PAIR, N_ALIBI_ROWS, L), lambda b, g: (g, 0, 0)),
                  pl.BlockSpec(memory_space=pltpu.SMEM)],
        out_specs=seq_spec,
        out_shape=jax.ShapeDtypeStruct((B, T, ATT_WIDTH), BF16),
        scratch_shapes=[pltpu.VMEM((nb, L, 2 * W), BF16),
                        pltpu.VMEM((HEAD_PAIR, HEAD_DIM + ONES_ROWS, T), BF16),
                        pltpu.VMEM((nb, W), F32),
                        pltpu.VMEM((ATTN_LOOKAHEAD + 1, 2 * W, L), BF16),
                        pltpu.VMEM((2, nb, L, L), F32),
                        pltpu.VMEM((2, T, L), BF16)],
        compiler_params=pltpu.CompilerParams(
            dimension_semantics=("arbitrary", "arbitrary"), vmem_limit_bytes=VMEM_LIMIT),
        name="moba_attention",
    )(q, k, v, kx, slope_rows, jnp.zeros((1,), jnp.int32))


def _ffn_kernel(x_ref, ylru_ref, yatt_ref, p_ref, wout_ref, lnf_ref, wup_ref, fw_ref, fb_ref, wdown_ref,
                lnp_ref, wg_ref, wp_ref, o_ref, gbuf, gcar):
    t = pl.program_id(1)
    tt = x_ref.shape[1]
    d_ff = wdown_ref.shape[0]
    pad = V7X_SUBLANES

    @pl.when(t == 0)
    def _():
        gcar[...] = jnp.zeros_like(gcar)

    def chain(r):
        rows = slice(r * FFN_SUB_ROWS, (r + 1) * FFN_SUB_ROWS)
        sub = FFN_SUB_ROWS
        h = (x_ref[0, rows, :] + _dot(ylru_ref[0, rows, :], wout_ref[0:LRU_WIDTH, :])
             + _dot(yatt_ref[0, rows, :], wout_ref[LRU_WIDTH:LRU_WIDTH + ATT_WIDTH, :]))
        yield
        u = _rmsnorm(h, lnf_ref[...]).astype(BF16)
        yield
        acc = jnp.zeros_like(h)
        for c in range(d_ff // FF_CHUNK):
            lo = c * FF_CHUNK
            g = _dot(u, wup_ref[:, lo:lo + FF_CHUNK])
            vv = _dot(u, wup_ref[:, d_ff + lo:d_ff + lo + FF_CHUNK])
            gbuf[r, 0:pad, :] = gcar[c]
            gbuf[r, pad:pad + sub, :] = g
            gcar[c] = g[sub - pad:sub, :]
            gc = fb_ref[:, lo:lo + FF_CHUNK] + fw_ref[FFN_CONV - 1:FFN_CONV, lo:lo + FF_CHUNK] * g
            for kk in range(FFN_CONV - 1):
                gc = gc + fw_ref[kk:kk + 1, lo:lo + FF_CHUNK] * gbuf[r, pl.ds(pad - (FFN_CONV - 1) + kk, sub), :]
            act = (jax.nn.gelu(gc) * vv).astype(BF16)
            acc = acc + _dot(act, wdown_ref[lo:lo + FF_CHUNK, :])
            yield
        h = h + acc
        un = _rmsnorm(h, lnp_ref[...]).astype(BF16)
        yield
        gate = jax.nn.sigmoid(_dot(un, wg_ref[...]))
        o_ref[0, rows, :] = h + gate * _dot(p_ref[0, rows, :].astype(BF16), wp_ref[...])
        yield

    _run_skewed([chain(r) for r in range(tt // FFN_SUB_ROWS)], n_phase=4 + d_ff // FF_CHUNK)


def _ffn_call(x, ylru, yatt, p, wout, lnf, wup, fw, fb, wdown, lnp, wg, wp):
    B, T, D = x.shape
    tt = min(ROW_TILE, T)
    d_ff = wdown.shape[0]
    row_spec = lambda w: pl.BlockSpec((1, tt, w), lambda b, t: (b, t, 0))
    consts = (wout, lnf, wup, fw, fb, wdown, lnp, wg, wp)
    return pl.pallas_call(
        _ffn_kernel,
        grid=(B, T // tt),
        in_specs=[row_spec(D), row_spec(LRU_WIDTH), row_spec(ATT_WIDTH), row_spec(p.shape[-1])]
                 + [_const_spec(c.shape) for c in consts],
        out_specs=row_spec(D),
        out_shape=jax.ShapeDtypeStruct((B, T, D), x.dtype),
        scratch_shapes=[pltpu.VMEM((tt // FFN_SUB_ROWS, FFN_SUB_ROWS + V7X_SUBLANES, FF_CHUNK), F32),
                        pltpu.VMEM((d_ff // FF_CHUNK, V7X_SUBLANES, FF_CHUNK), F32)],
        compiler_params=pltpu.CompilerParams(
            dimension_semantics=("arbitrary", "arbitrary"), vmem_limit_bytes=VMEM_LIMIT),
        name="outproj_ffn_ple",
    )(x, ylru, yatt, p, *consts)


def _block_diag(w):
    n, c, d = w.shape
    return jnp.einsum("ncd,nm->ncmd", w, jnp.eye(n, dtype=w.dtype)).reshape(n * c, n * d)


def _gate_weights(wa, wx):
    per = V7X_MXU_DIM // LRU_BLOCK_DIM
    groups = []
    for hf in range(LRU_WIDTH // V7X_MXU_DIM):
        sl = slice(hf * per, (hf + 1) * per)
        groups.append(jnp.concatenate([_block_diag(wa[sl]), _block_diag(wx[sl])], axis=1))
    return jnp.stack(groups).astype(BF16)


def _split3_bf16(s):
    s1 = s.astype(BF16).astype(F32)
    s2 = (s - s1).astype(BF16).astype(F32)
    s3 = (s - s1 - s2).astype(BF16).astype(F32)
    return s1, s2, s3


def _attention_constants(T):
    nb = T // MOBA_BLOCK
    W = HEAD_PAIR * HEAD_DIM
    pos = np.arange(T, dtype=np.int32)
    onehot = (pos[:, None] // MOBA_BLOCK == np.arange(nb)[None, :]).astype(np.float32)
    hi = ((pos >> 6) << 6).astype(np.float32)[:, None]
    lo = (pos & 63).astype(np.float32)[:, None]
    cols = [onehot, hi, hi, hi, lo, lo, lo]
    kx = np.concatenate(cols + [np.zeros((T, W - nb - 6), np.float32)], axis=1)
    slopes = np.float32(2.0) ** (np.float32(-8.0) * np.arange(1, N_HEADS + 1, dtype=np.float32) / np.float32(N_HEADS))
    s1, s2, s3 = _split3_bf16(jnp.asarray(slopes * np.float32(LOG2E)))
    rows = jnp.stack([s1, s2, s3, s1, s2, s3] + [jnp.zeros_like(s1)] * (N_ALIBI_ROWS - 6), axis=1)
    slope_rows = jnp.broadcast_to(rows[:, :, None], (N_HEADS, N_ALIBI_ROWS, MOBA_BLOCK)).astype(F32)
    return jnp.asarray(kx, dtype=BF16), slope_rows


def kernel(x, p, ln_mix, w_in, conv_w, conv_b, gate_a_w, gate_a_b, gate_x_w, gate_x_b, lru_lambda, q_gain,
           k_gain, w_out, ln_ffn, w_up, ffn_conv_w, ffn_conv_b, w_down, ln_ple, w_ple_gate, w_ple_proj):
    B, T, D = x.shape
    depth = w_in.shape[0]
    assert T % MOBA_BLOCK == 0 and T % min(ROW_TILE, T) == 0
    row = lambda a: a.reshape(1, -1)
    seg = jnp.asarray(np.kron(np.eye(V7X_MXU_DIM // HEAD_DIM), np.ones((HEAD_DIM, HEAD_DIM))), dtype=BF16)
    kx, slope_rows = _attention_constants(T)
    h = x
    for i in range(depth):
        ylru, q, k, v = _inproj_call(
            h, row(ln_mix[i]), w_in[i].astype(BF16), conv_w[i], row(conv_b[i]),
            _gate_weights(gate_a_w[i], gate_x_w[i]), row(gate_a_b[i]), row(gate_x_b[i]), row(lru_lambda[i]),
            row(jnp.tile(q_gain[i], N_HEADS)), row(jnp.tile(k_gain[i], N_HEADS)), seg)
        yatt = _attn_call(q, k, v, kx, slope_rows)
        h = _ffn_call(
            h, ylru, yatt, p[i].astype(h.dtype), w_out[i].astype(BF16), row(ln_ffn[i]), w_up[i].astype(BF16),
            ffn_conv_w[i], row(ffn_conv_b[i]), w_down[i].astype(BF16), row(ln_ple[i]),
            w_ple_gate[i].astype(BF16), w_ple_proj[i].astype(BF16))
    return h
```

```python
import functools

import numpy as np
import jax
import jax.numpy as jnp
from jax import lax
from jax.experimental import pallas as pl
from jax.experimental.pallas import tpu as pltpu

F32 = jnp.float32
BF16 = jnp.bfloat16

LRU_WIDTH = 512
LRU_BLOCKS = 8
LRU_BLOCK_DIM = LRU_WIDTH // LRU_BLOCKS
LRU_CONV = 4
LRU_C = 8.0
N_HEADS = 8
HEAD_DIM = 64
ATT_WIDTH = N_HEADS * HEAD_DIM
MOBA_BLOCK = 256
MOBA_TOPK = 3
FFN_CONV = 3
EPS = 1e-6
NEG = -1e30
LOG2E = float(np.log2(np.e))

V7X_LANES = 128
V7X_SUBLANES = 8
V7X_MXU_DIM = 256
V7X_VMEM_BYTES = 64 * 1024 * 1024

ROW_TILE = 512
HEAD_PAIR = V7X_LANES // HEAD_DIM
FF_CHUNK = 1536
FFN_SUB_ROWS = 256
N_ALIBI_ROWS = V7X_SUBLANES
ONES_ROWS = 16
ATTN_LOOKAHEAD = 2
VMEM_LIMIT = 56 * 1024 * 1024


def _const_spec(shape):
    nd = len(shape)
    return pl.BlockSpec(shape, lambda *_: (0,) * nd, pipeline_mode=pl.Buffered(1))


def _rmsnorm(x, g):
    return x * lax.rsqrt(jnp.mean(x * x, axis=-1, keepdims=True) + EPS) * g


def _dot(a, b):
    return jnp.dot(a, b, preferred_element_type=F32)


def _run_skewed(chains, n_phase):
    for step in range(n_phase + len(chains) - 1):
        for r, ch in enumerate(chains):
            if 0 <= step - r < n_phase:
                next(ch)


def _segment_sumsq(z, seg_ref):
    sq = z * z
    parts = []
    for hf in range(z.shape[1] // V7X_MXU_DIM):
        s = sq[:, hf * V7X_MXU_DIM:(hf + 1) * V7X_MXU_DIM]
        hi = s.astype(BF16)
        lo = (s - hi.astype(F32)).astype(BF16)
        parts.append(_dot(hi, seg_ref[...]) + _dot(lo, seg_ref[...]))
    return jnp.concatenate(parts, axis=1)


def _linear_scan(a, b, h_in):
    n, c = a.shape
    sl = V7X_SUBLANES
    a3 = a.reshape(n // sl, sl, c)
    b3 = b.reshape(n // sl, sl, c)
    sub = lax.broadcasted_iota(jnp.int32, a3.shape, 1)
    s = 1
    while s < sl:
        keep = sub >= s
        a_sh = jnp.where(keep, pltpu.roll(a3, s, 1), 1.0)
        b_sh = jnp.where(keep, pltpu.roll(b3, s, 1), 0.0)
        b3 = b3 + a3 * b_sh
        a3 = a3 * a_sh
        s *= 2
    carry = h_in
    groups = []
    for g in range(n // sl):
        hg = b3[g] + a3[g] * carry
        groups.append(hg)
        carry = hg[sl - 1:sl, :]
    return jnp.concatenate(groups, axis=0)


def _inproj_kernel(x_ref, ln_ref, win_ref, cw_ref, cb_ref, wg_ref, ba_ref, bx_ref, lam_ref,
                   qg_ref, kg_ref, seg_ref,
                   ylru_ref, q_ref, k_ref, v_ref,
                   xbuf, hcarry):
    t = pl.program_id(1)
    tt = x_ref.shape[1]

    @pl.when(t == 0)
    def _():
        xbuf[0:V7X_SUBLANES, :] = jnp.zeros((V7X_SUBLANES, LRU_WIDTH), F32)
        hcarry[...] = jnp.zeros_like(hcarry)

    u = _rmsnorm(x_ref[0], ln_ref[...])
    proj = _dot(u.astype(BF16), win_ref[...])
    x_lru = proj[:, 0:LRU_WIDTH]
    g_lru = proj[:, LRU_WIDTH:2 * LRU_WIDTH]
    o = 2 * LRU_WIDTH
    q = proj[:, o:o + ATT_WIDTH]
    k = proj[:, o + ATT_WIDTH:o + 2 * ATT_WIDTH]
    v = proj[:, o + 2 * ATT_WIDTH:o + 3 * ATT_WIDTH]

    xbuf[V7X_SUBLANES:V7X_SUBLANES + tt, :] = x_lru
    xc = cb_ref[...] + cw_ref[LRU_CONV - 1:LRU_CONV, :] * x_lru
    for kk in range(LRU_CONV - 1):
        start = V7X_SUBLANES - (LRU_CONV - 1) + kk
        xc = xc + cw_ref[kk:kk + 1, :] * xbuf[pl.ds(start, tt), :]
    xbuf[0:V7X_SUBLANES, :] = x_lru[tt - V7X_SUBLANES:tt, :]

    xcb = xc.astype(BF16)
    r_parts, i_parts = [], []
    for hf in range(LRU_WIDTH // V7X_MXU_DIM):
        z = _dot(xcb[:, hf * V7X_MXU_DIM:(hf + 1) * V7X_MXU_DIM], wg_ref[hf])
        r_parts.append(z[:, 0:V7X_MXU_DIM])
        i_parts.append(z[:, V7X_MXU_DIM:2 * V7X_MXU_DIM])
    r = jax.nn.sigmoid(jnp.concatenate(r_parts, axis=1) + ba_ref[...])
    ig = jax.nn.sigmoid(jnp.concatenate(i_parts, axis=1) + bx_ref[...])

    nl = -lam_ref[...]
    softplus = jnp.maximum(nl, 0.0) + jnp.log(1.0 + jnp.exp(-jnp.abs(nl)))
    log_a = (-LRU_C) * r * softplus
    a = jnp.exp(log_a)
    z = 1.0 - a * a
    mult = jnp.where(z > 0.0, z * lax.rsqrt(z), 0.0)
    gx = ig * xc
    b = mult * gx
    first = lax.broadcasted_iota(jnp.int32, (V7X_SUBLANES, LRU_WIDTH), 0) + t * tt == 0
    b = jnp.concatenate([jnp.where(first, gx[0:V7X_SUBLANES, :], b[0:V7X_SUBLANES, :]), b[V7X_SUBLANES:, :]], axis=0)

    h = _linear_scan(a, b, hcarry[...])
    hcarry[...] = h[tt - 1:tt, :]
    ylru_ref[0] = (h * jax.nn.gelu(g_lru)).astype(ylru_ref.dtype)

    qn = q * lax.rsqrt(_segment_sumsq(q, seg_ref) * (1.0 / HEAD_DIM) + EPS) * qg_ref[...]
    kn = k * lax.rsqrt(_segment_sumsq(k, seg_ref) * (1.0 / HEAD_DIM) + EPS) * kg_ref[...]
    q_ref[0] = qn.astype(q_ref.dtype)
    k_ref[0] = kn.astype(k_ref.dtype)
    v_ref[0] = v.astype(v_ref.dtype)


def _inproj_call(x, ln, win, cw, cb, wg, ba, bx, lam, qg, kg, seg):
    B, T, D = x.shape
    tt = min(ROW_TILE, T)
    row_spec = lambda w: pl.BlockSpec((1, tt, w), lambda b, t: (b, t, 0))
    consts = (ln, win, cw, cb, wg, ba, bx, lam, qg, kg, seg)
    return pl.pallas_call(
        _inproj_kernel,
        grid=(B, T // tt),
        in_specs=[row_spec(D)] + [_const_spec(c.shape) for c in consts],
        out_specs=[row_spec(LRU_WIDTH), row_spec(ATT_WIDTH), row_spec(ATT_WIDTH), row_spec(ATT_WIDTH)],
        out_shape=[jax.ShapeDtypeStruct((B, T, LRU_WIDTH), BF16),
                   jax.ShapeDtypeStruct((B, T, ATT_WIDTH), F32),
                   jax.ShapeDtypeStruct((B, T, ATT_WIDTH), F32),
                   jax.ShapeDtypeStruct((B, T, ATT_WIDTH), BF16)],
        scratch_shapes=[pltpu.VMEM((tt + V7X_SUBLANES, LRU_WIDTH), F32),
                        pltpu.VMEM((1, LRU_WIDTH), F32)],
        compiler_params=pltpu.CompilerParams(
            dimension_semantics=("arbitrary", "arbitrary"), vmem_limit_bytes=VMEM_LIMIT),
        name="inproj_rglru",
    )(x, *consts)


def _select_bias(gate, i):
    nb = gate.shape[0]
    rowi = lax.broadcasted_iota(jnp.int32, gate.shape, 0)
    rowf = rowi.astype(F32)
    ninf = -jnp.inf
    g = jnp.where(rowi < i, gate, ninf)
    sel = rowi == i
    for _ in range(min(MOBA_TOPK, i)):
        mx = jnp.max(g, axis=0, keepdims=True)
        cand = jnp.logical_and(g == mx, g > ninf)
        first = jnp.min(jnp.where(cand, rowf, float(nb)), axis=0, keepdims=True)
        pick = jnp.logical_and(rowf == first, cand)
        sel = jnp.logical_or(sel, pick)
        g = jnp.where(pick, ninf, g)
    return jnp.where(sel, 0.0, NEG)


def _attn_kernel(q_ref, k_ref, v_ref, kx_ref, slope_ref, zero_ref, o_ref,
                 kp_ref, vt_ref, km_ref, qp_ref, s_ref, p_ref):
    L = MOBA_BLOCK
    nb = k_ref.shape[1] // L
    W = HEAD_PAIR * HEAD_DIM
    rows8 = (L // V7X_SUBLANES, V7X_SUBLANES, L)

    lane1 = lax.broadcasted_iota(jnp.int32, (1, W), 1)
    for j in range(nb):
        kj = k_ref[0, j * L:(j + 1) * L, :]
        kp_ref[j, :, 0:W] = kj.astype(BF16)
        kp_ref[j, :, W:2 * W] = kx_ref[j * L:(j + 1) * L, :]
        kmean = jnp.mean(kj, axis=0, keepdims=True)
        for hh in range(HEAD_PAIR):
            in_head = jnp.logical_and(lane1 >= hh * HEAD_DIM, lane1 < (hh + 1) * HEAD_DIM)
            km_ref[hh * nb + j:hh * nb + j + 1, :] = jnp.where(in_head, kmean, 0.0)
        vt = v_ref[0, j * L:(j + 1) * L, :].astype(F32).T.astype(BF16)
        for hh in range(HEAD_PAIR):
            vt_ref[hh, 0:HEAD_DIM, j * L:(j + 1) * L] = vt[hh * HEAD_DIM:(hh + 1) * HEAD_DIM, :]
    for hh in range(HEAD_PAIR):
        vt_ref[hh, HEAD_DIM:HEAD_DIM + ONES_ROWS, :] = jnp.ones((ONES_ROWS, nb * L), BF16)

    causal = lax.broadcasted_iota(jnp.int32, (L, L), 0) <= lax.broadcasted_iota(jnp.int32, (L, L), 1)
    pad_rows = jnp.zeros((W - nb - N_ALIBI_ROWS, L), F32)
    qscale = LOG2E / np.sqrt(HEAD_DIM)

    dyn0 = zero_ref[0]

    shared = {}
    qrow = lax.broadcasted_iota(jnp.int32, (W, L), 0)

    class Stream:
        pass

    def prepare(n, i, hh):
        st = Stream()
        st.i, st.hh, st.slot, st.qslot = i, hh, n % 2, n % qp_ref.shape[0]
        if hh == 0:
            qf = q_ref[0, i * L:(i + 1) * L, :]
            shared["gate"] = lax.dot_general(km_ref[...], qf, (((1,), (1,)), ((), ())),
                                             precision=lax.Precision.HIGHEST, preferred_element_type=F32)
            shared["qt"] = (qf * qscale).T
        gate = shared["gate"][hh * nb:(hh + 1) * nb, :]
        qt = jnp.where(jnp.logical_and(qrow >= hh * HEAD_DIM, qrow < (hh + 1) * HEAD_DIM), shared["qt"], 0.0)
        qp_ref[st.qslot] = jnp.concatenate([qt, _select_bias(gate, i), slope_ref[hh], pad_rows],
                                           axis=0).astype(BF16)
        st.m8 = None
        return st

    def score_step(st, c):
        s = _dot(kp_ref[c], qp_ref[st.qslot])
        if c == st.i:
            s = jnp.where(causal, s, NEG)
        s_ref[st.slot + dyn0, c] = s
        m8 = jnp.max(s.reshape(rows8), axis=0)
        st.m8 = m8 if st.m8 is None else jnp.maximum(st.m8, m8)

    def prob_step(st, c):
        p_ref[st.slot, c * L:(c + 1) * L, :] = jnp.exp2(s_ref[st.slot + dyn0, c] - st.m).astype(BF16)

    def finish(st):
        k_len = (st.i + 1) * L
        acc = _dot(vt_ref[st.hh, :, 0:k_len], p_ref[st.slot, 0:k_len, :])
        return acc[0:HEAD_DIM, :] / acc[HEAD_DIM:HEAD_DIM + 1, :]

    def order(i):
        return [i] + list(range(i))

    streams = [(i, hh) for i in range(nb) for hh in range(HEAD_PAIR)]
    outs = []
    lookahead = qp_ref.shape[0] - 1
    ready = [prepare(n, *streams[n]) for n in range(min(lookahead, len(streams)))]

    def emit_pv(st):
        outs.append(finish(st))
        if st.hh == HEAD_PAIR - 1:
            o_ref[0, st.i * L:(st.i + 1) * L, :] = jnp.concatenate(outs, axis=0).T.astype(o_ref.dtype)
            outs.clear()

    for n in range(len(streams) + 2):
        if n + lookahead < len(streams):
            ready.append(prepare(n + lookahead, *streams[n + lookahead]))
        st = ready[n] if n < len(streams) else None
        pr = ready[n - 1] if 0 <= n - 1 < len(streams) else None
        pv = ready[n - 2] if 0 <= n - 2 < len(streams) else None
        cur_steps = order(st.i) if st is not None else []
        prev_steps = order(pr.i) if pr is not None else []
        n_steps = max(len(cur_steps), len(prev_steps))
        for k in range(n_steps):
            if pv is not None and k == n_steps // 2:
                emit_pv(pv)
                pv = None
            if k < len(cur_steps):
                score_step(st, cur_steps[k])
            if k < len(prev_steps):
                prob_step(pr, prev_steps[k])
        if pv is not None:
            emit_pv(pv)
        if st is not None:
            st.m = jnp.max(st.m8, axis=0, keepdims=True)


def _attn_call(q, k, v, kx, slope_rows):
    B, T, _ = q.shape
    L = MOBA_BLOCK
    nb = T // L
    W = HEAD_PAIR * HEAD_DIM
    npairs = N_HEADS // HEAD_PAIR
    seq_spec = pl.BlockSpec((1, T, W), lambda b, g: (b, 0, g))
    return pl.pallas_call(
        _attn_kernel,
        grid=(B, npairs),
        in_specs=[seq_spec, seq_spec, seq_spec, _const_spec(kx.shape),
                  pl.BlockSpec((HEAD_PAIR, N_ALIBI_ROWS, L), lambda b, g: (g, 0, 0)),
                  pl.BlockSpec(memory_space=pltpu.SMEM)],
        out_specs=seq_spec,
        out_shape=jax.ShapeDtypeStruct((B, T, ATT_WIDTH), BF16),
        scratch_shapes=[pltpu.VMEM((nb, L, 2 * W), BF16),
                        pltpu.VMEM((HEAD_PAIR, HEAD_DIM + ONES_ROWS, T), BF16),
                        pltpu.VMEM((HEAD_PAIR * nb, W), F32),
                        pltpu.VMEM((ATTN_LOOKAHEAD + 1, 2 * W, L), BF16),
                        pltpu.VMEM((2, nb, L, L), F32),
                        pltpu.VMEM((2, T, L), BF16)],
        compiler_params=pltpu.CompilerParams(
            dimension_semantics=("arbitrary", "arbitrary"), vmem_limit_bytes=VMEM_LIMIT),
        name="moba_attention",
    )(q, k, v, kx, slope_rows, jnp.zeros((1,), jnp.int32))


def _ffn_kernel(x_ref, ylru_ref, yatt_ref, p_ref, wout_ref, lnf_ref, wup_ref, fw_ref, fb_ref, wdown_ref,
                lnp_ref, wg_ref, wp_ref, o_ref, gbuf, gcar):
    t = pl.program_id(1)
    tt = x_ref.shape[1]
    d_ff = wdown_ref.shape[0]
    pad = V7X_SUBLANES

    @pl.when(t == 0)
    def _():
        gcar[...] = jnp.zeros_like(gcar)

    def chain(r):
        rows = slice(r * FFN_SUB_ROWS, (r + 1) * FFN_SUB_ROWS)
        sub = FFN_SUB_ROWS
        h = (x_ref[0, rows, :] + _dot(ylru_ref[0, rows, :], wout_ref[0:LRU_WIDTH, :])
             + _dot(yatt_ref[0, rows, :], wout_ref[LRU_WIDTH:LRU_WIDTH + ATT_WIDTH, :]))
        yield
        u = _rmsnorm(h, lnf_ref[...]).astype(BF16)
        yield
        acc = jnp.zeros_like(h)
        for c in range(d_ff // FF_CHUNK):
            lo = c * FF_CHUNK
            g = _dot(u, wup_ref[:, lo:lo + FF_CHUNK])
            vv = _dot(u, wup_ref[:, d_ff + lo:d_ff + lo + FF_CHUNK])
            gbuf[r, 0:pad, :] = gcar[c]
            gbuf[r, pad:pad + sub, :] = g
            gcar[c] = g[sub - pad:sub, :]
            gc = fb_ref[:, lo:lo + FF_CHUNK] + fw_ref[FFN_CONV - 1:FFN_CONV, lo:lo + FF_CHUNK] * g
            for kk in range(FFN_CONV - 1):
                gc = gc + fw_ref[kk:kk + 1, lo:lo + FF_CHUNK] * gbuf[r, pl.ds(pad - (FFN_CONV - 1) + kk, sub), :]
            act = (jax.nn.gelu(gc) * vv).astype(BF16)
            acc = acc + _dot(act, wdown_ref[lo:lo + FF_CHUNK, :])
            yield
        h = h + acc
        un = _rmsnorm(h, lnp_ref[...]).astype(BF16)
        yield
        gate = jax.nn.sigmoid(_dot(un, wg_ref[...]))
        o_ref[0, rows, :] = h + gate * _dot(p_ref[0, rows, :].astype(BF16), wp_ref[...])
        yield

    _run_skewed([chain(r) for r in range(tt // FFN_SUB_ROWS)], n_phase=4 + d_ff // FF_CHUNK)


def _ffn_call(x, ylru, yatt, p, wout, lnf, wup, fw, fb, wdown, lnp, wg, wp):
    B, T, D = x.shape
    tt = min(ROW_TILE, T)
    d_ff = wdown.shape[0]
    row_spec = lambda w: pl.BlockSpec((1, tt, w), lambda b, t: (b, t, 0))
    consts = (wout, lnf, wup, fw, fb, wdown, lnp, wg, wp)
    return pl.pallas_call(
        _ffn_kernel,
        grid=(B, T // tt),
        in_specs=[row_spec(D), row_spec(LRU_WIDTH), row_spec(ATT_WIDTH), row_spec(p.shape[-1])]
                 + [_const_spec(c.shape) for c in consts],
        out_specs=row_spec(D),
        out_shape=jax.ShapeDtypeStruct((B, T, D), x.dtype),
        scratch_shapes=[pltpu.VMEM((tt // FFN_SUB_ROWS, FFN_SUB_ROWS + V7X_SUBLANES, FF_CHUNK), F32),
                        pltpu.VMEM((d_ff // FF_CHUNK, V7X_SUBLANES, FF_CHUNK), F32)],
        compiler_params=pltpu.CompilerParams(
            dimension_semantics=("arbitrary", "arbitrary"), vmem_limit_bytes=VMEM_LIMIT),
        name="outproj_ffn_ple",
    )(x, ylru, yatt, p, *consts)


def _block_diag(w):
    n, c, d = w.shape
    return jnp.einsum("ncd,nm->ncmd", w, jnp.eye(n, dtype=w.dtype)).reshape(n * c, n * d)


def _gate_weights(wa, wx):
    per = V7X_MXU_DIM // LRU_BLOCK_DIM
    groups = []
    for hf in range(LRU_WIDTH // V7X_MXU_DIM):
        sl = slice(hf * per, (hf + 1) * per)
        groups.append(jnp.concatenate([_block_diag(wa[sl]), _block_diag(wx[sl])], axis=1))
    return jnp.stack(groups).astype(BF16)


def _split3_bf16(s):
    s1 = s.astype(BF16).astype(F32)
    s2 = (s - s1).astype(BF16).astype(F32)
    s3 = (s - s1 - s2).astype(BF16).astype(F32)
    return s1, s2, s3


def _attention_constants(T):
    nb = T // MOBA_BLOCK
    W = HEAD_PAIR * HEAD_DIM
    pos = np.arange(T, dtype=np.int32)
    onehot = (pos[:, None] // MOBA_BLOCK == np.arange(nb)[None, :]).astype(np.float32)
    hi = ((pos >> 6) << 6).astype(np.float32)[:, None]
    lo = (pos & 63).astype(np.float32)[:, None]
    cols = [onehot, hi, hi, hi, lo, lo, lo]
    kx = np.concatenate(cols + [np.zeros((T, W - nb - 6), np.float32)], axis=1)
    slopes = np.float32(2.0) ** (np.float32(-8.0) * np.arange(1, N_HEADS + 1, dtype=np.float32) / np.float32(N_HEADS))
    s1, s2, s3 = _split3_bf16(jnp.asarray(slopes * np.float32(LOG2E)))
    rows = jnp.stack([s1, s2, s3, s1, s2, s3] + [jnp.zeros_like(s1)] * (N_ALIBI_ROWS - 6), axis=1)
    slope_rows = jnp.broadcast_to(rows[:, :, None], (N_HEADS, N_ALIBI_ROWS, MOBA_BLOCK)).astype(F32)
    return jnp.asarray(kx, dtype=BF16), slope_rows


def kernel(x, p, ln_mix, w_in, conv_w, conv_b, gate_a_w, gate_a_b, gate_x_w, gate_x_b, lru_lambda, q_gain,
           k_gain, w_out, ln_ffn, w_up, ffn_conv_w, ffn_conv_b, w_down, ln_ple, w_ple_gate, w_ple_proj):
    B, T, D = x.shape
    depth = w_in.shape[0]
    assert T % MOBA_BLOCK == 0 and T % min(ROW_TILE, T) == 0
    row = lambda a: a.reshape(1, -1)
    seg = jnp.asarray(np.kron(np.eye(V7X_MXU_DIM // HEAD_DIM), np.ones((HEAD_DIM, HEAD_DIM))), dtype=BF16)
    kx, slope_rows = _attention_constants(T)
    h = x
    for i in range(depth):
        ylru, q, k, v = _inproj_call(
            h, row(ln_mix[i]), w_in[i].astype(BF16), conv_w[i], row(conv_b[i]),
            _gate_weights(gate_a_w[i], gate_x_w[i]), row(gate_a_b[i]), row(gate_x_b[i]), row(lru_lambda[i]),
            row(jnp.tile(q_gain[i], N_HEADS)), row(jnp.tile(k_gain[i], N_HEADS)), seg)
        yatt = _attn_call(q, k, v, kx, slope_rows)
        h = _ffn_call(
            h, ylru, yatt, p[i].astype(h.dtype), w_out[i].astype(BF16), row(ln_ffn[i]), w_up[i].astype(BF16),
            ffn_conv_w[i], row(ffn_conv_b[i]), w_down[i].astype(BF16), row(ln_ple[i]),
            w_ple_gate[i].astype(BF16), w_ple_proj[i].astype(BF16))
    return h
```

```python
import functools

import numpy as np
import jax
import jax.numpy as jnp
from jax import lax
from jax.experimental import pallas as pl
from jax.experimental.pallas import tpu as pltpu

F32 = jnp.float32
BF16 = jnp.bfloat16

LRU_WIDTH = 512
LRU_BLOCKS = 8
LRU_BLOCK_DIM = LRU_WIDTH // LRU_BLOCKS
LRU_CONV = 4
LRU_C = 8.0
N_HEADS = 8
HEAD_DIM = 64
ATT_WIDTH = N_HEADS * HEAD_DIM
MOBA_BLOCK = 256
MOBA_TOPK = 3
FFN_CONV = 3
EPS = 1e-6
NEG = -1e30
LOG2E = float(np.log2(np.e))

V7X_LANES = 128
V7X_SUBLANES = 8
V7X_MXU_DIM = 256
V7X_VMEM_BYTES = 64 * 1024 * 1024

ROW_TILE = 512
HEAD_PAIR = V7X_LANES // HEAD_DIM
FF_CHUNK = 1536
FFN_SUB_ROWS = 256
N_ALIBI_ROWS = V7X_SUBLANES
ONES_ROWS = 16
ATTN_LOOKAHEAD = 2
VMEM_LIMIT = 56 * 1024 * 1024


def _const_spec(shape):
    nd = len(shape)
    return pl.BlockSpec(shape, lambda *_: (0,) * nd, pipeline_mode=pl.Buffered(1))


def _rmsnorm(x, g):
    return x * lax.rsqrt(jnp.mean(x * x, axis=-1, keepdims=True) + EPS) * g


def _dot(a, b):
    return jnp.dot(a, b, preferred_element_type=F32)


def _run_skewed(chains, n_phase):
    for step in range(n_phase + len(chains) - 1):
        for r, ch in enumerate(chains):
            if 0 <= step - r < n_phase:
                next(ch)


def _segment_sumsq(z, seg_ref):
    sq = z * z
    parts = []
    for hf in range(z.shape[1] // V7X_MXU_DIM):
        s = sq[:, hf * V7X_MXU_DIM:(hf + 1) * V7X_MXU_DIM]
        hi = s.astype(BF16)
        lo = (s - hi.astype(F32)).astype(BF16)
        parts.append(_dot(hi, seg_ref[...]) + _dot(lo, seg_ref[...]))
    return jnp.concatenate(parts, axis=1)


def _linear_scan(a, b, h_in):
    n, c = a.shape
    sl = V7X_SUBLANES
    a3 = a.reshape(n // sl, sl, c)
    b3 = b.reshape(n // sl, sl, c)
    sub = lax.broadcasted_iota(jnp.int32, a3.shape, 1)
    s = 1
    while s < sl:
        keep = sub >= s
        a_sh = jnp.where(keep, pltpu.roll(a3, s, 1), 1.0)
        b_sh = jnp.where(keep, pltpu.roll(b3, s, 1), 0.0)
        b3 = b3 + a3 * b_sh
        a3 = a3 * a_sh
        s *= 2
    carry = h_in
    groups = []
    for g in range(n // sl):
        hg = b3[g] + a3[g] * carry
        groups.append(hg)
        carry = hg[sl - 1:sl, :]
    return jnp.concatenate(groups, axis=0)


def _inproj_kernel(x_ref, ln_ref, win_ref, cw_ref, cb_ref, wg_ref, ba_ref, bx_ref, lam_ref,
                   qg_ref, kg_ref, seg_ref,
                   ylru_ref, q_ref, k_ref, v_ref,
                   xbuf, hcarry):
    t = pl.program_id(1)
    tt = x_ref.shape[1]

    @pl.when(t == 0)
    def _():
        xbuf[0:V7X_SUBLANES, :] = jnp.zeros((V7X_SUBLANES, LRU_WIDTH), F32)
        hcarry[...] = jnp.zeros_like(hcarry)

    u = _rmsnorm(x_ref[0], ln_ref[...])
    proj = _dot(u.astype(BF16), win_ref[...])
    x_lru = proj[:, 0:LRU_WIDTH]
    g_lru = proj[:, LRU_WIDTH:2 * LRU_WIDTH]
    o = 2 * LRU_WIDTH
    q = proj[:, o:o + ATT_WIDTH]
    k = proj[:, o + ATT_WIDTH:o + 2 * ATT_WIDTH]
    v = proj[:, o + 2 * ATT_WIDTH:o + 3 * ATT_WIDTH]

    xbuf[V7X_SUBLANES:V7X_SUBLANES + tt, :] = x_lru
    xc = cb_ref[...] + cw_ref[LRU_CONV - 1:LRU_CONV, :] * x_lru
    for kk in range(LRU_CONV - 1):
        start = V7X_SUBLANES - (LRU_CONV - 1) + kk
        xc = xc + cw_ref[kk:kk + 1, :] * xbuf[pl.ds(start, tt), :]
    xbuf[0:V7X_SUBLANES, :] = x_lru[tt - V7X_SUBLANES:tt, :]

    xcb = xc.astype(BF16)
    r_parts, i_parts = [], []
    for hf in range(LRU_WIDTH // V7X_MXU_DIM):
        z = _dot(xcb[:, hf * V7X_MXU_DIM:(hf + 1) * V7X_MXU_DIM], wg_ref[hf])
        r_parts.append(z[:, 0:V7X_MXU_DIM])
        i_parts.append(z[:, V7X_MXU_DIM:2 * V7X_MXU_DIM])
    r = jax.nn.sigmoid(jnp.concatenate(r_parts, axis=1) + ba_ref[...])
    ig = jax.nn.sigmoid(jnp.concatenate(i_parts, axis=1) + bx_ref[...])

    nl = -lam_ref[...]
    softplus = jnp.maximum(nl, 0.0) + jnp.log(1.0 + jnp.exp(-jnp.abs(nl)))
    log_a = (-LRU_C) * r * softplus
    a = jnp.exp(log_a)
    z = 1.0 - a * a
    mult = jnp.where(z > 0.0, z * lax.rsqrt(z), 0.0)
    gx = ig * xc
    b = mult * gx
    first = lax.broadcasted_iota(jnp.int32, (V7X_SUBLANES, LRU_WIDTH), 0) + t * tt == 0
    b = jnp.concatenate([jnp.where(first, gx[0:V7X_SUBLANES, :], b[0:V7X_SUBLANES, :]), b[V7X_SUBLANES:, :]], axis=0)

    h = _linear_scan(a, b, hcarry[...])
    hcarry[...] = h[tt - 1:tt, :]
    ylru_ref[0] = (h * jax.nn.gelu(g_lru)).astype(ylru_ref.dtype)

    qn = q * lax.rsqrt(_segment_sumsq(q, seg_ref) * (1.0 / HEAD_DIM) + EPS) * qg_ref[...]
    kn = k * lax.rsqrt(_segment_sumsq(k, seg_ref) * (1.0 / HEAD_DIM) + EPS) * kg_ref[...]
    q_ref[0] = qn.astype(q_ref.dtype)
    k_ref[0] = kn.astype(k_ref.dtype)
    v_ref[0] = v.astype(v_ref.dtype)


def _inproj_call(x, ln, win, cw, cb, wg, ba, bx, lam, qg, kg, seg):
    B, T, D = x.shape
    tt = min(ROW_TILE, T)
    row_spec = lambda w: pl.BlockSpec((1, tt, w), lambda b, t: (b, t, 0))
    consts = (ln, win, cw, cb, wg, ba, bx, lam, qg, kg, seg)
    return pl.pallas_call(
        _inproj_kernel,
        grid=(B, T // tt),
        in_specs=[row_spec(D)] + [_const_spec(c.shape) for c in consts],
        out_specs=[row_spec(LRU_WIDTH), row_spec(ATT_WIDTH), row_spec(ATT_WIDTH), row_spec(ATT_WIDTH)],
        out_shape=[jax.ShapeDtypeStruct((B, T, LRU_WIDTH), BF16),
                   jax.ShapeDtypeStruct((B, T, ATT_WIDTH), F32),
                   jax.ShapeDtypeStruct((B, T, ATT_WIDTH), F32),
                   jax.ShapeDtypeStruct((B, T, ATT_WIDTH), BF16)],
        scratch_shapes=[pltpu.VMEM((tt + V7X_SUBLANES, LRU_WIDTH), F32),
                        pltpu.VMEM((1, LRU_WIDTH), F32)],
        compiler_params=pltpu.CompilerParams(
            dimension_semantics=("arbitrary", "arbitrary"), vmem_limit_bytes=VMEM_LIMIT),
        name="inproj_rglru",
    )(x, *consts)


def _select_bias(gate, i):
    nb = gate.shape[0]
    rowi = lax.broadcasted_iota(jnp.int32, gate.shape, 0)
    rowf = rowi.astype(F32)
    ninf = -jnp.inf
    g = jnp.where(rowi < i, gate, ninf)
    sel = rowi == i
    for _ in range(min(MOBA_TOPK, i)):
        mx = jnp.max(g, axis=0, keepdims=True)
        cand = jnp.logical_and(g == mx, g > ninf)
        first = jnp.min(jnp.where(cand, rowf, float(nb)), axis=0, keepdims=True)
        pick = jnp.logical_and(rowf == first, cand)
        sel = jnp.logical_or(sel, pick)
        g = jnp.where(pick, ninf, g)
    return jnp.where(sel, 0.0, NEG)


def _attn_kernel(q_ref, k_ref, v_ref, kx_ref, slope_ref, zero_ref, o_ref,
                 kp_ref, vt_ref, km_ref, qp_ref, s_ref, p_ref):
    L = MOBA_BLOCK
    nb = k_ref.shape[1] // L
    W = HEAD_PAIR * HEAD_DIM
    rows8 = (L // V7X_SUBLANES, V7X_SUBLANES, L)

    lane1 = lax.broadcasted_iota(jnp.int32, (1, W), 1)
    for j in range(nb):
        kj = k_ref[0, j * L:(j + 1) * L, :]
        kp_ref[j, :, 0:W] = kj.astype(BF16)
        kp_ref[j, :, W:2 * W] = kx_ref[j * L:(j + 1) * L, :]
        kmean = jnp.mean(kj, axis=0, keepdims=True)
        for hh in range(HEAD_PAIR):
            in_head = jnp.logical_and(lane1 >= hh * HEAD_DIM, lane1 < (hh + 1) * HEAD_DIM)
            km_ref[hh * nb + j:hh * nb + j + 1, :] = jnp.where(in_head, kmean, 0.0)
        vt = v_ref[0, j * L:(j + 1) * L, :].astype(F32).T.astype(BF16)
        for hh in range(HEAD_PAIR):
            vt_ref[hh, 0:HEAD_DIM, j * L:(j + 1) * L] = vt[hh * HEAD_DIM:(hh + 1) * HEAD_DIM, :]
    for hh in range(HEAD_PAIR):
        vt_ref[hh, HEAD_DIM:HEAD_DIM + ONES_ROWS, :] = jnp.ones((ONES_ROWS, nb * L), BF16)

    causal = lax.broadcasted_iota(jnp.int32, (L, L), 0) <= lax.broadcasted_iota(jnp.int32, (L, L), 1)
    pad_rows = jnp.zeros((W - nb - N_ALIBI_ROWS, L), F32)
    qscale = LOG2E / np.sqrt(HEAD_DIM)

    dyn0 = zero_ref[0]

    shared = {}
    qrow = lax.broadcasted_iota(jnp.int32, (W, L), 0)

    class Stream:
        pass

    def prepare(n, i, hh):
        st = Stream()
        st.i, st.hh, st.slot, st.qslot = i, hh, n % 2, n % qp_ref.shape[0]
        if hh == 0:
            qf = q_ref[0, i * L:(i + 1) * L, :]
            qt_raw = qf.T
            shared["gate"] = lax.dot_general(km_ref[...], qt_raw, (((1,), (0,)), ((), ())),
                                             precision=lax.Precision.HIGHEST, preferred_element_type=F32)
            shared["qt"] = qt_raw * qscale
        gate = shared["gate"][hh * nb:(hh + 1) * nb, :]
        qt = jnp.where(jnp.logical_and(qrow >= hh * HEAD_DIM, qrow < (hh + 1) * HEAD_DIM), shared["qt"], 0.0)
        qp_ref[st.qslot] = jnp.concatenate([qt, _select_bias(gate, i), slope_ref[hh], pad_rows],
                                           axis=0).astype(BF16)
        st.m8 = None
        return st

    def score_step(st, c):
        s = _dot(kp_ref[c], qp_ref[st.qslot])
        if c == st.i:
            s = jnp.where(causal, s, NEG)
        s_ref[st.slot + dyn0, c] = s
        m8 = jnp.max(s.reshape(rows8), axis=0)
        st.m8 = m8 if st.m8 is None else jnp.maximum(st.m8, m8)

    def prob_step(st, c):
        p_ref[st.slot, c * L:(c + 1) * L, :] = jnp.exp2(s_ref[st.slot + dyn0, c] - st.m).astype(BF16)

    def finish(st):
        k_len = (st.i + 1) * L
        acc = _dot(vt_ref[st.hh, :, 0:k_len], p_ref[st.slot, 0:k_len, :])
        return acc[0:HEAD_DIM, :] / acc[HEAD_DIM:HEAD_DIM + 1, :]

    def order(i):
        return [i] + list(range(i))

    streams = [(i, hh) for i in range(nb) for hh in range(HEAD_PAIR)]
    outs = []
    lookahead = qp_ref.shape[0] - 1
    ready = [prepare(n, *streams[n]) for n in range(min(lookahead, len(streams)))]

    def emit_pv(st):
        outs.append(finish(st))
        if st.hh == HEAD_PAIR - 1:
            o_ref[0, st.i * L:(st.i + 1) * L, :] = jnp.concatenate(outs, axis=0).T.astype(o_ref.dtype)
            outs.clear()

    for n in range(len(streams) + 2):
        if n + lookahead < len(streams):
            ready.append(prepare(n + lookahead, *streams[n + lookahead]))
        st = ready[n] if n < len(streams) else None
        pr = ready[n - 1] if 0 <= n - 1 < len(streams) else None
        pv = ready[n - 2] if 0 <= n - 2 < len(streams) else None
        cur_steps = order(st.i) if st is not None else []
        prev_steps = order(pr.i) if pr is not None else []
        n_steps = max(len(cur_steps), len(prev_steps))
        for k in range(n_steps):
            if pv is not None and k == n_steps // 2:
                emit_pv(pv)
                pv = None
            if k < len(cur_steps):
                score_step(st, cur_steps[k])
            if k < len(prev_steps):
                prob_step(pr, prev_steps[k])
        if pv is not None:
            emit_pv(pv)
        if st is not None:
            st.m = jnp.max(st.m8, axis=0, keepdims=True)


def _attn_call(q, k, v, kx, slope_rows):
    B, T, _ = q.shape
    L = MOBA_BLOCK
    nb = T // L
    W = HEAD_PAIR * HEAD_DIM
    npairs = N_HEADS // HEAD_PAIR
    seq_spec = pl.BlockSpec((1, T, W), lambda b, g: (b, 0, g))
    return pl.pallas_call(
        _attn_kernel,
        grid=(B, npairs),
        in_specs=[seq_spec, seq_spec, seq_spec, _const_spec(kx.shape),
                  pl.BlockSpec((HEAD_PAIR, N_ALIBI_ROWS, L), lambda b, g: (g, 0, 0)),
                  pl.BlockSpec(memory_space=pltpu.SMEM)],
        out_specs=seq_spec,
        out_shape=jax.ShapeDtypeStruct((B, T, ATT_WIDTH), BF16),
        scratch_shapes=[pltpu.VMEM((nb, L, 2 * W), BF16),
                        pltpu.VMEM((HEAD_PAIR, HEAD_DIM + ONES_ROWS, T), BF16),
                        pltpu.VMEM((HEAD_PAIR * nb, W), F32),
                        pltpu.VMEM((ATTN_LOOKAHEAD + 1, 2 * W, L), BF16),
                        pltpu.VMEM((2, nb, L, L), F32),
                        pltpu.VMEM((2, T, L), BF16)],
        compiler_params=pltpu.CompilerParams(
            dimension_semantics=("arbitrary", "arbitrary"), vmem_limit_bytes=VMEM_LIMIT),
        name="moba_attention",
    )(q, k, v, kx, slope_rows, jnp.zeros((1,), jnp.int32))


def _ffn_kernel(x_ref, ylru_ref, yatt_ref, p_ref, wout_ref, lnf_ref, wup_ref, fw_ref, fb_ref, wdown_ref,
                lnp_ref, wg_ref, wp_ref, o_ref, gbuf, gcar):
    t = pl.program_id(1)
    tt = x_ref.shape[1]
    d_ff = wdown_ref.shape[0]
    pad = V7X_SUBLANES

    @pl.when(t == 0)
    def _():
        gcar[...] = jnp.zeros_like(gcar)

    def chain(r):
        rows = slice(r * FFN_SUB_ROWS, (r + 1) * FFN_SUB_ROWS)
        sub = FFN_SUB_ROWS
        h = (x_ref[0, rows, :] + _dot(ylru_ref[0, rows, :], wout_ref[0:LRU_WIDTH, :])
             + _dot(yatt_ref[0, rows, :], wout_ref[LRU_WIDTH:LRU_WIDTH + ATT_WIDTH, :]))
        yield
        u = _rmsnorm(h, lnf_ref[...]).astype(BF16)
        yield
        acc = jnp.zeros_like(h)
        for c in range(d_ff // FF_CHUNK):
            lo = c * FF_CHUNK
            g = _dot(u, wup_ref[:, lo:lo + FF_CHUNK])
            vv = _dot(u, wup_ref[:, d_ff + lo:d_ff + lo + FF_CHUNK])
            gbuf[r, 0:pad, :] = gcar[c]
            gbuf[r, pad:pad + sub, :] = g
            gcar[c] = g[sub - pad:sub, :]
            gc = fb_ref[:, lo:lo + FF_CHUNK] + fw_ref[FFN_CONV - 1:FFN_CONV, lo:lo + FF_CHUNK] * g
            for kk in range(FFN_CONV - 1):
                gc = gc + fw_ref[kk:kk + 1, lo:lo + FF_CHUNK] * gbuf[r, pl.ds(pad - (FFN_CONV - 1) + kk, sub), :]
            act = (jax.nn.gelu(gc) * vv).astype(BF16)
            acc = acc + _dot(act, wdown_ref[lo:lo + FF_CHUNK, :])
            yield
        h = h + acc
        un = _rmsnorm(h, lnp_ref[...]).astype(BF16)
        yield
        gate = jax.nn.sigmoid(_dot(un, wg_ref[...]))
        o_ref[0, rows, :] = h + gate * _dot(p_ref[0, rows, :].astype(BF16), wp_ref[...])
        yield

    _run_skewed([chain(r) for r in range(tt // FFN_SUB_ROWS)], n_phase=4 + d_ff // FF_CHUNK)


def _ffn_call(x, ylru, yatt, p, wout, lnf, wup, fw, fb, wdown, lnp, wg, wp):
    B, T, D = x.shape
    tt = min(ROW_TILE, T)
    d_ff = wdown.shape[0]
    row_spec = lambda w: pl.BlockSpec((1, tt, w), lambda b, t: (b, t, 0))
    consts = (wout, lnf, wup, fw, fb, wdown, lnp, wg, wp)
    return pl.pallas_call(
        _ffn_kernel,
        grid=(B, T // tt),
        in_specs=[row_spec(D), row_spec(LRU_WIDTH), row_spec(ATT_WIDTH), row_spec(p.shape[-1])]
                 + [_const_spec(c.shape) for c in consts],
        out_specs=row_spec(D),
        out_shape=jax.ShapeDtypeStruct((B, T, D), x.dtype),
        scratch_shapes=[pltpu.VMEM((tt // FFN_SUB_ROWS, FFN_SUB_ROWS + V7X_SUBLANES, FF_CHUNK), F32),
                        pltpu.VMEM((d_ff // FF_CHUNK, V7X_SUBLANES, FF_CHUNK), F32)],
        compiler_params=pltpu.CompilerParams(
            dimension_semantics=("arbitrary", "arbitrary"), vmem_limit_bytes=VMEM_LIMIT),
        name="outproj_ffn_ple",
    )(x, ylru, yatt, p, *consts)


def _block_diag(w):
    n, c, d = w.shape
    return jnp.einsum("ncd,nm->ncmd", w, jnp.eye(n, dtype=w.dtype)).reshape(n * c, n * d)


def _gate_weights(wa, wx):
    per = V7X_MXU_DIM // LRU_BLOCK_DIM
    groups = []
    for hf in range(LRU_WIDTH // V7X_MXU_DIM):
        sl = slice(hf * per, (hf + 1) * per)
        groups.append(jnp.concatenate([_block_diag(wa[sl]), _block_diag(wx[sl])], axis=1))
    return jnp.stack(groups).astype(BF16)


def _split3_bf16(s):
    s1 = s.astype(BF16).astype(F32)
    s2 = (s - s1).astype(BF16).astype(F32)
    s3 = (s - s1 - s2).astype(BF16).astype(F32)
    return s1, s2, s3


def _attention_constants(T):
    nb = T // MOBA_BLOCK
    W = HEAD_PAIR * HEAD_DIM
    pos = np.arange(T, dtype=np.int32)
    onehot = (pos[:, None] // MOBA_BLOCK == np.arange(nb)[None, :]).astype(np.float32)
    hi = ((pos >> 6) << 6).astype(np.float32)[:, None]
    lo = (pos & 63).astype(np.float32)[:, None]
    cols = [onehot, hi, hi, hi, lo, lo, lo]
    kx = np.concatenate(cols + [np.zeros((T, W - nb - 6), np.float32)], axis=1)
    slopes = np.float32(2.0) ** (np.float32(-8.0) * np.arange(1, N_HEADS + 1, dtype=np.float32) / np.float32(N_HEADS))
    s1, s2, s3 = _split3_bf16(jnp.asarray(slopes * np.float32(LOG2E)))
    rows = jnp.stack([s1, s2, s3, s1, s2, s3] + [jnp.zeros_like(s1)] * (N_ALIBI_ROWS - 6), axis=1)
    slope_rows = jnp.broadcast_to(rows[:, :, None], (N_HEADS, N_ALIBI_ROWS, MOBA_BLOCK)).astype(F32)
    return jnp.asarray(kx, dtype=BF16), slope_rows


def kernel(x, p, ln_mix, w_in, conv_w, conv_b, gate_a_w, gate_a_b, gate_x_w, gate_x_b, lru_lambda, q_gain,
           k_gain, w_out, ln_ffn, w_up, ffn_conv_w, ffn_conv_b, w_down, ln_ple, w_ple_gate, w_ple_proj):
    B, T, D = x.shape
    depth = w_in.shape[0]
    assert T % MOBA_BLOCK == 0 and T % min(ROW_TILE, T) == 0
    row = lambda a: a.reshape(1, -1)
    seg = jnp.asarray(np.kron(np.eye(V7X_MXU_DIM // HEAD_DIM), np.ones((HEAD_DIM, HEAD_DIM))), dtype=BF16)
    kx, slope_rows = _attention_constants(T)
    h = x
    for i in range(depth):
        ylru, q, k, v = _inproj_call(
            h, row(ln_mix[i]), w_in[i].astype(BF16), conv_w[i], row(conv_b[i]),
            _gate_weights(gate_a_w[i], gate_x_w[i]), row(gate_a_b[i]), row(gate_x_b[i]), row(lru_lambda[i]),
            row(jnp.tile(q_gain[i], N_HEADS)), row(jnp.tile(k_gain[i], N_HEADS)), seg)
        yatt = _attn_call(q, k, v, kx, slope_rows)
        h = _ffn_call(
            h, ylru, yatt, p[i].astype(h.dtype), w_out[i].astype(BF16), row(ln_ffn[i]), w_up[i].astype(BF16),
            ffn_conv_w[i], row(ffn_conv_b[i]), w_down[i].astype(BF16), row(ln_ple[i]),
            w_ple_gate[i].astype(BF16), w_ple_proj[i].astype(BF16))
    return h
```

```python
import functools

import numpy as np
import jax
import jax.numpy as jnp
from jax import lax
from jax.experimental import pallas as pl
from jax.experimental.pallas import tpu as pltpu

F32 = jnp.float32
BF16 = jnp.bfloat16

LRU_WIDTH = 512
LRU_BLOCKS = 8
LRU_BLOCK_DIM = LRU_WIDTH // LRU_BLOCKS
LRU_CONV = 4
LRU_C = 8.0
N_HEADS = 8
HEAD_DIM = 64
ATT_WIDTH = N_HEADS * HEAD_DIM
MOBA_BLOCK = 256
MOBA_TOPK = 3
FFN_CONV = 3
EPS = 1e-6
NEG = -1e30
LOG2E = float(np.log2(np.e))

V7X_LANES = 128
V7X_SUBLANES = 8
V7X_MXU_DIM = 256
V7X_VMEM_BYTES = 64 * 1024 * 1024

ROW_TILE = 512
HEAD_PAIR = V7X_LANES // HEAD_DIM
FF_CHUNK = 1536
FFN_SUB_ROWS = 256
N_ALIBI_ROWS = V7X_SUBLANES
ONES_ROWS = 16
ATTN_LOOKAHEAD = 2
VMEM_LIMIT = 56 * 1024 * 1024


def _const_spec(shape):
    nd = len(shape)
    return pl.BlockSpec(shape, lambda *_: (0,) * nd, pipeline_mode=pl.Buffered(1))


def _rmsnorm(x, g):
    return x * lax.rsqrt(jnp.mean(x * x, axis=-1, keepdims=True) + EPS) * g


def _dot(a, b):
    return jnp.dot(a, b, preferred_element_type=F32)


def _split2_bf16(a):
    hi = a.astype(BF16)
    return hi, (a - hi.astype(F32)).astype(BF16)


def _dot_split(a, b):
    a_hi, a_lo = _split2_bf16(a)
    b_hi, b_lo = _split2_bf16(b)
    m = a.shape[0]
    both = _dot(jnp.concatenate([a_hi, a_lo], axis=0), b_hi)
    return both[0:m] + both[m:2 * m] + _dot(a_hi, b_lo)


def _run_skewed(chains, n_phase):
    for step in range(n_phase + len(chains) - 1):
        for r, ch in enumerate(chains):
            if 0 <= step - r < n_phase:
                next(ch)


def _segment_sumsq(z, seg_ref):
    sq = z * z
    parts = []
    for hf in range(z.shape[1] // V7X_MXU_DIM):
        s = sq[:, hf * V7X_MXU_DIM:(hf + 1) * V7X_MXU_DIM]
        hi = s.astype(BF16)
        lo = (s - hi.astype(F32)).astype(BF16)
        parts.append(_dot(hi, seg_ref[...]) + _dot(lo, seg_ref[...]))
    return jnp.concatenate(parts, axis=1)


def _linear_scan(a, b, h_in):
    n, c = a.shape
    sl = V7X_SUBLANES
    a3 = a.reshape(n // sl, sl, c)
    b3 = b.reshape(n // sl, sl, c)
    sub = lax.broadcasted_iota(jnp.int32, a3.shape, 1)
    s = 1
    while s < sl:
        keep = sub >= s
        a_sh = jnp.where(keep, pltpu.roll(a3, s, 1), 1.0)
        b_sh = jnp.where(keep, pltpu.roll(b3, s, 1), 0.0)
        b3 = b3 + a3 * b_sh
        a3 = a3 * a_sh
        s *= 2
    carry = h_in
    groups = []
    for g in range(n // sl):
        hg = b3[g] + a3[g] * carry
        groups.append(hg)
        carry = hg[sl - 1:sl, :]
    return jnp.concatenate(groups, axis=0)


def _inproj_kernel(x_ref, ln_ref, win_ref, cw_ref, cb_ref, wg_ref, ba_ref, bx_ref, lam_ref,
                   qg_ref, kg_ref, seg_ref,
                   ylru_ref, q_ref, k_ref, v_ref,
                   xbuf, hcarry):
    t = pl.program_id(1)
    tt = x_ref.shape[1]

    @pl.when(t == 0)
    def _():
        xbuf[0:V7X_SUBLANES, :] = jnp.zeros((V7X_SUBLANES, LRU_WIDTH), F32)
        hcarry[...] = jnp.zeros_like(hcarry)

    u = _rmsnorm(x_ref[0], ln_ref[...])
    proj = _dot(u.astype(BF16), win_ref[...])
    x_lru = proj[:, 0:LRU_WIDTH]
    g_lru = proj[:, LRU_WIDTH:2 * LRU_WIDTH]
    o = 2 * LRU_WIDTH
    q = proj[:, o:o + ATT_WIDTH]
    k = proj[:, o + ATT_WIDTH:o + 2 * ATT_WIDTH]
    v = proj[:, o + 2 * ATT_WIDTH:o + 3 * ATT_WIDTH]

    xbuf[V7X_SUBLANES:V7X_SUBLANES + tt, :] = x_lru
    xc = cb_ref[...] + cw_ref[LRU_CONV - 1:LRU_CONV, :] * x_lru
    for kk in range(LRU_CONV - 1):
        start = V7X_SUBLANES - (LRU_CONV - 1) + kk
        xc = xc + cw_ref[kk:kk + 1, :] * xbuf[pl.ds(start, tt), :]
    xbuf[0:V7X_SUBLANES, :] = x_lru[tt - V7X_SUBLANES:tt, :]

    xcb = xc.astype(BF16)
    r_parts, i_parts = [], []
    for hf in range(LRU_WIDTH // V7X_MXU_DIM):
        z = _dot(xcb[:, hf * V7X_MXU_DIM:(hf + 1) * V7X_MXU_DIM], wg_ref[hf])
        r_parts.append(z[:, 0:V7X_MXU_DIM])
        i_parts.append(z[:, V7X_MXU_DIM:2 * V7X_MXU_DIM])
    r = jax.nn.sigmoid(jnp.concatenate(r_parts, axis=1) + ba_ref[...])
    ig = jax.nn.sigmoid(jnp.concatenate(i_parts, axis=1) + bx_ref[...])

    nl = -lam_ref[...]
    softplus = jnp.maximum(nl, 0.0) + jnp.log(1.0 + jnp.exp(-jnp.abs(nl)))
    log_a = (-LRU_C) * r * softplus
    a = jnp.exp(log_a)
    z = 1.0 - a * a
    mult = jnp.where(z > 0.0, z * lax.rsqrt(z), 0.0)
    gx = ig * xc
    b = mult * gx
    first = lax.broadcasted_iota(jnp.int32, (V7X_SUBLANES, LRU_WIDTH), 0) + t * tt == 0
    b = jnp.concatenate([jnp.where(first, gx[0:V7X_SUBLANES, :], b[0:V7X_SUBLANES, :]), b[V7X_SUBLANES:, :]], axis=0)

    h = _linear_scan(a, b, hcarry[...])
    hcarry[...] = h[tt - 1:tt, :]
    ylru_ref[0] = (h * jax.nn.gelu(g_lru)).astype(ylru_ref.dtype)

    qn = q * lax.rsqrt(_segment_sumsq(q, seg_ref) * (1.0 / HEAD_DIM) + EPS) * qg_ref[...]
    kn = k * lax.rsqrt(_segment_sumsq(k, seg_ref) * (1.0 / HEAD_DIM) + EPS) * kg_ref[...]
    q_ref[0] = qn.astype(q_ref.dtype)
    k_ref[0] = kn.astype(k_ref.dtype)
    v_ref[0] = v.astype(v_ref.dtype)


def _inproj_call(x, ln, win, cw, cb, wg, ba, bx, lam, qg, kg, seg):
    B, T, D = x.shape
    tt = min(ROW_TILE, T)
    row_spec = lambda w: pl.BlockSpec((1, tt, w), lambda b, t: (b, t, 0))
    consts = (ln, win, cw, cb, wg, ba, bx, lam, qg, kg, seg)
    return pl.pallas_call(
        _inproj_kernel,
        grid=(B, T // tt),
        in_specs=[row_spec(D)] + [_const_spec(c.shape) for c in consts],
        out_specs=[row_spec(LRU_WIDTH), row_spec(ATT_WIDTH), row_spec(ATT_WIDTH), row_spec(ATT_WIDTH)],
        out_shape=[jax.ShapeDtypeStruct((B, T, LRU_WIDTH), BF16),
                   jax.ShapeDtypeStruct((B, T, ATT_WIDTH), F32),
                   jax.ShapeDtypeStruct((B, T, ATT_WIDTH), F32),
                   jax.ShapeDtypeStruct((B, T, ATT_WIDTH), BF16)],
        scratch_shapes=[pltpu.VMEM((tt + V7X_SUBLANES, LRU_WIDTH), F32),
                        pltpu.VMEM((1, LRU_WIDTH), F32)],
        compiler_params=pltpu.CompilerParams(
            dimension_semantics=("arbitrary", "arbitrary"), vmem_limit_bytes=VMEM_LIMIT),
        name="inproj_rglru",
    )(x, *consts)


def _select_bias(gate, i):
    nb = gate.shape[0]
    rowi = lax.broadcasted_iota(jnp.int32, gate.shape, 0)
    rowf = rowi.astype(F32)
    ninf = -jnp.inf
    g = jnp.where(rowi < i, gate, ninf)
    sel = rowi == i
    for _ in range(min(MOBA_TOPK, i)):
        mx = jnp.max(g, axis=0, keepdims=True)
        cand = jnp.logical_and(g == mx, g > ninf)
        first = jnp.min(jnp.where(cand, rowf, float(nb)), axis=0, keepdims=True)
        pick = jnp.logical_and(rowf == first, cand)
        sel = jnp.logical_or(sel, pick)
        g = jnp.where(pick, ninf, g)
    return jnp.where(sel, 0.0, NEG)


def _attn_kernel(q_ref, k_ref, v_ref, kx_ref, slope_ref, zero_ref, o_ref,
                 kp_ref, vt_ref, km_ref, qp_ref, s_ref, p_ref):
    L = MOBA_BLOCK
    nb = k_ref.shape[1] // L
    W = HEAD_PAIR * HEAD_DIM
    rows8 = (L // V7X_SUBLANES, V7X_SUBLANES, L)

    lane1 = lax.broadcasted_iota(jnp.int32, (1, W), 1)
    for j in range(nb):
        kj = k_ref[0, j * L:(j + 1) * L, :]
        kp_ref[j, :, 0:W] = kj.astype(BF16)
        kp_ref[j, :, W:2 * W] = kx_ref[j * L:(j + 1) * L, :]
        kmean = jnp.mean(kj, axis=0, keepdims=True)
        for hh in range(HEAD_PAIR):
            in_head = jnp.logical_and(lane1 >= hh * HEAD_DIM, lane1 < (hh + 1) * HEAD_DIM)
            km_ref[hh * nb + j:hh * nb + j + 1, :] = jnp.where(in_head, kmean, 0.0)
        vt = v_ref[0, j * L:(j + 1) * L, :].astype(F32).T.astype(BF16)
        for hh in range(HEAD_PAIR):
            vt_ref[hh, 0:HEAD_DIM, j * L:(j + 1) * L] = vt[hh * HEAD_DIM:(hh + 1) * HEAD_DIM, :]
    for hh in range(HEAD_PAIR):
        vt_ref[hh, HEAD_DIM:HEAD_DIM + ONES_ROWS, :] = jnp.ones((ONES_ROWS, nb * L), BF16)

    causal = lax.broadcasted_iota(jnp.int32, (L, L), 0) <= lax.broadcasted_iota(jnp.int32, (L, L), 1)
    pad_rows = jnp.zeros((W - nb - N_ALIBI_ROWS, L), F32)
    qscale = LOG2E / np.sqrt(HEAD_DIM)

    dyn0 = zero_ref[0]

    shared = {}
    qrow = lax.broadcasted_iota(jnp.int32, (W, L), 0)

    class Stream:
        pass

    def prepare(n, i, hh):
        st = Stream()
        st.i, st.hh, st.slot, st.qslot = i, hh, n % 2, n % qp_ref.shape[0]
        if hh == 0:
            qf = q_ref[0, i * L:(i + 1) * L, :]
            qt_raw = qf.T
            shared["gate"] = _dot_split(km_ref[...], qt_raw)
            shared["qt"] = qt_raw * qscale
        gate = shared["gate"][hh * nb:(hh + 1) * nb, :]
        qt = jnp.where(jnp.logical_and(qrow >= hh * HEAD_DIM, qrow < (hh + 1) * HEAD_DIM), shared["qt"], 0.0)
        qp_ref[st.qslot] = jnp.concatenate([qt, _select_bias(gate, i), slope_ref[hh], pad_rows],
                                           axis=0).astype(BF16)
        st.m8 = None
        return st

    def score_step(st, c):
        s = _dot(kp_ref[c], qp_ref[st.qslot])
        if c == st.i:
            s = jnp.where(causal, s, NEG)
        s_ref[st.slot + dyn0, c] = s
        m8 = jnp.max(s.reshape(rows8), axis=0)
        st.m8 = m8 if st.m8 is None else jnp.maximum(st.m8, m8)

    def prob_step(st, c):
        p_ref[st.slot, c * L:(c + 1) * L, :] = jnp.exp2(s_ref[st.slot + dyn0, c] - st.m).astype(BF16)

    def finish(st):
        k_len = (st.i + 1) * L
        acc = _dot(vt_ref[st.hh, :, 0:k_len], p_ref[st.slot, 0:k_len, :])
        return acc[0:HEAD_DIM, :] / acc[HEAD_DIM:HEAD_DIM + 1, :]

    def order(i):
        return [i] + list(range(i))

    streams = [(i, hh) for i in range(nb) for hh in range(HEAD_PAIR)]
    outs = []
    lookahead = qp_ref.shape[0] - 1
    ready = [prepare(n, *streams[n]) for n in range(min(lookahead, len(streams)))]

    def emit_pv(st):
        outs.append(finish(st))
        if st.hh == HEAD_PAIR - 1:
            o_ref[0, st.i * L:(st.i + 1) * L, :] = jnp.concatenate(outs, axis=0).T.astype(o_ref.dtype)
            outs.clear()

    for n in range(len(streams) + 2):
        if n + lookahead < len(streams):
            ready.append(prepare(n + lookahead, *streams[n + lookahead]))
        st = ready[n] if n < len(streams) else None
        pr = ready[n - 1] if 0 <= n - 1 < len(streams) else None
        pv = ready[n - 2] if 0 <= n - 2 < len(streams) else None
        cur_steps = order(st.i) if st is not None else []
        prev_steps = order(pr.i) if pr is not None else []
        n_steps = max(len(cur_steps), len(prev_steps))
        for k in range(n_steps):
            if pv is not None and k == n_steps // 2:
                emit_pv(pv)
                pv = None
            if k < len(cur_steps):
                score_step(st, cur_steps[k])
            if k < len(prev_steps):
                prob_step(pr, prev_steps[k])
        if pv is not None:
            emit_pv(pv)
        if st is not None:
            st.m = jnp.max(st.m8, axis=0, keepdims=True)


def _attn_call(q, k, v, kx, slope_rows):
    B, T, _ = q.shape
    L = MOBA_BLOCK
    nb = T // L
    W = HEAD_PAIR * HEAD_DIM
    npairs = N_HEADS // HEAD_PAIR
    seq_spec = pl.BlockSpec((1, T, W), lambda b, g: (b, 0, g))
    return pl.pallas_call(
        _attn_kernel,
        grid=(B, npairs),
        in_specs=[seq_spec, seq_spec, seq_spec, _const_spec(kx.shape),
                  pl.BlockSpec((HEAD_PAIR, N_ALIBI_ROWS, L), lambda b, g: (g, 0, 0)),
                  pl.BlockSpec(memory_space=pltpu.SMEM)],
        out_specs=seq_spec,
        out_shape=jax.ShapeDtypeStruct((B, T, ATT_WIDTH), BF16),
        scratch_shapes=[pltpu.VMEM((nb, L, 2 * W), BF16),
                        pltpu.VMEM((HEAD_PAIR, HEAD_DIM + ONES_ROWS, T), BF16),
                        pltpu.VMEM((HEAD_PAIR * nb, W), F32),
                        pltpu.VMEM((ATTN_LOOKAHEAD + 1, 2 * W, L), BF16),
                        pltpu.VMEM((2, nb, L, L), F32),
                        pltpu.VMEM((2, T, L), BF16)],
        compiler_params=pltpu.CompilerParams(
            dimension_semantics=("arbitrary", "arbitrary"), vmem_limit_bytes=VMEM_LIMIT),
        name="moba_attention",
    )(q, k, v, kx, slope_rows, jnp.zeros((1,), jnp.int32))


def _ffn_kernel(x_ref, ylru_ref, yatt_ref, p_ref, wout_ref, lnf_ref, wup_ref, fw_ref, fb_ref, wdown_ref,
                lnp_ref, wg_ref, wp_ref, o_ref, gbuf, gcar):
    t = pl.program_id(1)
    tt = x_ref.shape[1]
    d_ff = wdown_ref.shape[0]
    pad = V7X_SUBLANES

    @pl.when(t == 0)
    def _():
        gcar[...] = jnp.zeros_like(gcar)

    def chain(r):
        rows = slice(r * FFN_SUB_ROWS, (r + 1) * FFN_SUB_ROWS)
        sub = FFN_SUB_ROWS
        h = (x_ref[0, rows, :] + _dot(ylru_ref[0, rows, :], wout_ref[0:LRU_WIDTH, :])
             + _dot(yatt_ref[0, rows, :], wout_ref[LRU_WIDTH:LRU_WIDTH + ATT_WIDTH, :]))
        yield
        u = _rmsnorm(h, lnf_ref[...]).astype(BF16)
        yield
        acc = jnp.zeros_like(h)
        for c in range(d_ff // FF_CHUNK):
            lo = c * FF_CHUNK
            g = _dot(u, wup_ref[:, lo:lo + FF_CHUNK])
            vv = _dot(u, wup_ref[:, d_ff + lo:d_ff + lo + FF_CHUNK])
            gbuf[r, 0:pad, :] = gcar[c]
            gbuf[r, pad:pad + sub, :] = g
            gcar[c] = g[sub - pad:sub, :]
            gc = fb_ref[:, lo:lo + FF_CHUNK] + fw_ref[FFN_CONV - 1:FFN_CONV, lo:lo + FF_CHUNK] * g
            for kk in range(FFN_CONV - 1):
                gc = gc + fw_ref[kk:kk + 1, lo:lo + FF_CHUNK] * gbuf[r, pl.ds(pad - (FFN_CONV - 1) + kk, sub), :]
            act = (jax.nn.gelu(gc) * vv).astype(BF16)
            acc = acc + _dot(act, wdown_ref[lo:lo + FF_CHUNK, :])
            yield
        h = h + acc
        un = _rmsnorm(h, lnp_ref[...]).astype(BF16)
        yield
        gate = jax.nn.sigmoid(_dot(un, wg_ref[...]))
        o_ref[0, rows, :] = h + gate * _dot(p_ref[0, rows, :].astype(BF16), wp_ref[...])
        yield

    _run_skewed([chain(r) for r in range(tt // FFN_SUB_ROWS)], n_phase=4 + d_ff // FF_CHUNK)


def _ffn_call(x, ylru, yatt, p, wout, lnf, wup, fw, fb, wdown, lnp, wg, wp):
    B, T, D = x.shape
    tt = min(ROW_TILE, T)
    d_ff = wdown.shape[0]
    row_spec = lambda w: pl.BlockSpec((1, tt, w), lambda b, t: (b, t, 0))
    consts = (wout, lnf, wup, fw, fb, wdown, lnp, wg, wp)
    return pl.pallas_call(
        _ffn_kernel,
        grid=(B, T // tt),
        in_specs=[row_spec(D), row_spec(LRU_WIDTH), row_spec(ATT_WIDTH), row_spec(p.shape[-1])]
                 + [_const_spec(c.shape) for c in consts],
        out_specs=row_spec(D),
        out_shape=jax.ShapeDtypeStruct((B, T, D), x.dtype),
        scratch_shapes=[pltpu.VMEM((tt // FFN_SUB_ROWS, FFN_SUB_ROWS + V7X_SUBLANES, FF_CHUNK), F32),
                        pltpu.VMEM((d_ff // FF_CHUNK, V7X_SUBLANES, FF_CHUNK), F32)],
        compiler_params=pltpu.CompilerParams(
            dimension_semantics=("arbitrary", "arbitrary"), vmem_limit_bytes=VMEM_LIMIT),
        name="outproj_ffn_ple",
    )(x, ylru, yatt, p, *consts)


def _block_diag(w):
    n, c, d = w.shape
    return jnp.einsum("ncd,nm->ncmd", w, jnp.eye(n, dtype=w.dtype)).reshape(n * c, n * d)


def _gate_weights(wa, wx):
    per = V7X_MXU_DIM // LRU_BLOCK_DIM
    groups = []
    for hf in range(LRU_WIDTH // V7X_MXU_DIM):
        sl = slice(hf * per, (hf + 1) * per)
        groups.append(jnp.concatenate([_block_diag(wa[sl]), _block_diag(wx[sl])], axis=1))
    return jnp.stack(groups).astype(BF16)


def _split3_bf16(s):
    s1 = s.astype(BF16).astype(F32)
    s2 = (s - s1).astype(BF16).astype(F32)
    s3 = (s - s1 - s2).astype(BF16).astype(F32)
    return s1, s2, s3


def _attention_constants(T):
    nb = T // MOBA_BLOCK
    W = HEAD_PAIR * HEAD_DIM
    pos = np.arange(T, dtype=np.int32)
    onehot = (pos[:, None] // MOBA_BLOCK == np.arange(nb)[None, :]).astype(np.float32)
    hi = ((pos >> 6) << 6).astype(np.float32)[:, None]
    lo = (pos & 63).astype(np.float32)[:, None]
    cols = [onehot, hi, hi, hi, lo, lo, lo]
    kx = np.concatenate(cols + [np.zeros((T, W - nb - 6), np.float32)], axis=1)
    slopes = np.float32(2.0) ** (np.float32(-8.0) * np.arange(1, N_HEADS + 1, dtype=np.float32) / np.float32(N_HEADS))
    s1, s2, s3 = _split3_bf16(jnp.asarray(slopes * np.float32(LOG2E)))
    rows = jnp.stack([s1, s2, s3, s1, s2, s3] + [jnp.zeros_like(s1)] * (N_ALIBI_ROWS - 6), axis=1)
    slope_rows = jnp.broadcast_to(rows[:, :, None], (N_HEADS, N_ALIBI_ROWS, MOBA_BLOCK)).astype(F32)
    return jnp.asarray(kx, dtype=BF16), slope_rows


def kernel(x, p, ln_mix, w_in, conv_w, conv_b, gate_a_w, gate_a_b, gate_x_w, gate_x_b, lru_lambda, q_gain,
           k_gain, w_out, ln_ffn, w_up, ffn_conv_w, ffn_conv_b, w_down, ln_ple, w_ple_gate, w_ple_proj):
    B, T, D = x.shape
    depth = w_in.shape[0]
    assert T % MOBA_BLOCK == 0 and T % min(ROW_TILE, T) == 0
    row = lambda a: a.reshape(1, -1)
    seg = jnp.asarray(np.kron(np.eye(V7X_MXU_DIM // HEAD_DIM), np.ones((HEAD_DIM, HEAD_DIM))), dtype=BF16)
    kx, slope_rows = _attention_constants(T)
    h = x
    for i in range(depth):
        ylru, q, k, v = _inproj_call(
            h, row(ln_mix[i]), w_in[i].astype(BF16), conv_w[i], row(conv_b[i]),
            _gate_weights(gate_a_w[i], gate_x_w[i]), row(gate_a_b[i]), row(gate_x_b[i]), row(lru_lambda[i]),
            row(jnp.tile(q_gain[i], N_HEADS)), row(jnp.tile(k_gain[i], N_HEADS)), seg)
        yatt = _attn_call(q, k, v, kx, slope_rows)
        h = _ffn_call(
            h, ylru, yatt, p[i].astype(h.dtype), w_out[i].astype(BF16), row(ln_ffn[i]), w_up[i].astype(BF16),
            ffn_conv_w[i], row(ffn_conv_b[i]), w_down[i].astype(BF16), row(ln_ple[i]),
            w_ple_gate[i].astype(BF16), w_ple_proj[i].astype(BF16))
    return h
```

```python
import functools

import numpy as np
import jax
import jax.numpy as jnp
from jax import lax
from jax.experimental import pallas as pl
from jax.experimental.pallas import tpu as pltpu

F32 = jnp.float32
BF16 = jnp.bfloat16

LRU_WIDTH = 512
LRU_BLOCKS = 8
LRU_BLOCK_DIM = LRU_WIDTH // LRU_BLOCKS
LRU_CONV = 4
LRU_C = 8.0
N_HEADS = 8
HEAD_DIM = 64
ATT_WIDTH = N_HEADS * HEAD_DIM
MOBA_BLOCK = 256
MOBA_TOPK = 3
FFN_CONV = 3
EPS = 1e-6
NEG = -1e30
LOG2E = float(np.log2(np.e))

V7X_LANES = 128
V7X_SUBLANES = 8
V7X_MXU_DIM = 256
V7X_VMEM_BYTES = 64 * 1024 * 1024

ROW_TILE = 512
HEAD_PAIR = V7X_LANES // HEAD_DIM
FF_CHUNK = 3072
FFN_SUB_ROWS = 256
N_ALIBI_ROWS = V7X_SUBLANES
ONES_ROWS = 16
ATTN_LOOKAHEAD = 3
VMEM_LIMIT = 56 * 1024 * 1024


def _const_spec(shape):
    nd = len(shape)
    return pl.BlockSpec(shape, lambda *_: (0,) * nd, pipeline_mode=pl.Buffered(1))


def _rmsnorm(x, g):
    return x * lax.rsqrt(jnp.mean(x * x, axis=-1, keepdims=True) + EPS) * g


def _dot(a, b):
    return jnp.dot(a, b, preferred_element_type=F32)


def _split2_bf16(a):
    hi = a.astype(BF16)
    return hi, (a - hi.astype(F32)).astype(BF16)


def _dot_split(a, b):
    a_hi, a_lo = _split2_bf16(a)
    b_hi, b_lo = _split2_bf16(b)
    m = a.shape[0]
    both = _dot(jnp.concatenate([a_hi, a_lo], axis=0), b_hi)
    return both[0:m] + both[m:2 * m] + _dot(a_hi, b_lo)


def _run_skewed(chains, n_phase):
    for step in range(n_phase + len(chains) - 1):
        for r, ch in enumerate(chains):
            if 0 <= step - r < n_phase:
                next(ch)


def _segment_sumsq(z, seg_ref):
    sq = z * z
    parts = []
    for hf in range(z.shape[1] // V7X_MXU_DIM):
        s = sq[:, hf * V7X_MXU_DIM:(hf + 1) * V7X_MXU_DIM]
        hi = s.astype(BF16)
        lo = (s - hi.astype(F32)).astype(BF16)
        parts.append(_dot(hi, seg_ref[...]) + _dot(lo, seg_ref[...]))
    return jnp.concatenate(parts, axis=1)


def _linear_scan(a, b, h_in):
    n, c = a.shape
    sl = V7X_SUBLANES
    a3 = a.reshape(n // sl, sl, c)
    b3 = b.reshape(n // sl, sl, c)
    sub = lax.broadcasted_iota(jnp.int32, a3.shape, 1)
    s = 1
    while s < sl:
        keep = sub >= s
        a_sh = jnp.where(keep, pltpu.roll(a3, s, 1), 1.0)
        b_sh = jnp.where(keep, pltpu.roll(b3, s, 1), 0.0)
        b3 = b3 + a3 * b_sh
        a3 = a3 * a_sh
        s *= 2
    carry = h_in
    groups = []
    for g in range(n // sl):
        hg = b3[g] + a3[g] * carry
        groups.append(hg)
        carry = hg[sl - 1:sl, :]
    return jnp.concatenate(groups, axis=0)


def _inproj_kernel(x_ref, ln_ref, win_ref, cw_ref, cb_ref, wg_ref, ba_ref, bx_ref, lam_ref,
                   qg_ref, kg_ref, seg_ref,
                   ylru_ref, q_ref, k_ref, v_ref,
                   xbuf, hcarry):
    t = pl.program_id(1)
    tt = x_ref.shape[1]

    @pl.when(t == 0)
    def _():
        xbuf[0:V7X_SUBLANES, :] = jnp.zeros((V7X_SUBLANES, LRU_WIDTH), F32)
        hcarry[...] = jnp.zeros_like(hcarry)

    u = _rmsnorm(x_ref[0], ln_ref[...])
    proj = _dot(u.astype(BF16), win_ref[...])
    x_lru = proj[:, 0:LRU_WIDTH]
    g_lru = proj[:, LRU_WIDTH:2 * LRU_WIDTH]
    o = 2 * LRU_WIDTH
    q = proj[:, o:o + ATT_WIDTH]
    k = proj[:, o + ATT_WIDTH:o + 2 * ATT_WIDTH]
    v = proj[:, o + 2 * ATT_WIDTH:o + 3 * ATT_WIDTH]

    xbuf[V7X_SUBLANES:V7X_SUBLANES + tt, :] = x_lru
    xc = cb_ref[...] + cw_ref[LRU_CONV - 1:LRU_CONV, :] * x_lru
    for kk in range(LRU_CONV - 1):
        start = V7X_SUBLANES - (LRU_CONV - 1) + kk
        xc = xc + cw_ref[kk:kk + 1, :] * xbuf[pl.ds(start, tt), :]
    xbuf[0:V7X_SUBLANES, :] = x_lru[tt - V7X_SUBLANES:tt, :]

    xcb = xc.astype(BF16)
    r_parts, i_parts = [], []
    for hf in range(LRU_WIDTH // V7X_MXU_DIM):
        z = _dot(xcb[:, hf * V7X_MXU_DIM:(hf + 1) * V7X_MXU_DIM], wg_ref[hf])
        r_parts.append(z[:, 0:V7X_MXU_DIM])
        i_parts.append(z[:, V7X_MXU_DIM:2 * V7X_MXU_DIM])
    r = jax.nn.sigmoid(jnp.concatenate(r_parts, axis=1) + ba_ref[...])
    ig = jax.nn.sigmoid(jnp.concatenate(i_parts, axis=1) + bx_ref[...])

    nl = -lam_ref[...]
    softplus = jnp.maximum(nl, 0.0) + jnp.log(1.0 + jnp.exp(-jnp.abs(nl)))
    log_a = (-LRU_C) * r * softplus
    a = jnp.exp(log_a)
    z = 1.0 - a * a
    mult = jnp.where(z > 0.0, z * lax.rsqrt(z), 0.0)
    gx = ig * xc
    b = mult * gx
    first = lax.broadcasted_iota(jnp.int32, (V7X_SUBLANES, LRU_WIDTH), 0) + t * tt == 0
    b = jnp.concatenate([jnp.where(first, gx[0:V7X_SUBLANES, :], b[0:V7X_SUBLANES, :]), b[V7X_SUBLANES:, :]], axis=0)

    h = _linear_scan(a, b, hcarry[...])
    hcarry[...] = h[tt - 1:tt, :]
    ylru_ref[0] = (h * jax.nn.gelu(g_lru)).astype(ylru_ref.dtype)

    qn = q * lax.rsqrt(_segment_sumsq(q, seg_ref) * (1.0 / HEAD_DIM) + EPS) * qg_ref[...]
    kn = k * lax.rsqrt(_segment_sumsq(k, seg_ref) * (1.0 / HEAD_DIM) + EPS) * kg_ref[...]
    q_ref[0] = qn.astype(q_ref.dtype)
    k_ref[0] = kn.astype(k_ref.dtype)
    v_ref[0] = v.astype(v_ref.dtype)


def _inproj_call(x, ln, win, cw, cb, wg, ba, bx, lam, qg, kg, seg):
    B, T, D = x.shape
    tt = min(ROW_TILE, T)
    row_spec = lambda w: pl.BlockSpec((1, tt, w), lambda b, t: (b, t, 0))
    consts = (ln, win, cw, cb, wg, ba, bx, lam, qg, kg, seg)
    return pl.pallas_call(
        _inproj_kernel,
        grid=(B, T // tt),
        in_specs=[row_spec(D)] + [_const_spec(c.shape) for c in consts],
        out_specs=[row_spec(LRU_WIDTH), row_spec(ATT_WIDTH), row_spec(ATT_WIDTH), row_spec(ATT_WIDTH)],
        out_shape=[jax.ShapeDtypeStruct((B, T, LRU_WIDTH), BF16),
                   jax.ShapeDtypeStruct((B, T, ATT_WIDTH), F32),
                   jax.ShapeDtypeStruct((B, T, ATT_WIDTH), F32),
                   jax.ShapeDtypeStruct((B, T, ATT_WIDTH), BF16)],
        scratch_shapes=[pltpu.VMEM((tt + V7X_SUBLANES, LRU_WIDTH), F32),
                        pltpu.VMEM((1, LRU_WIDTH), F32)],
        compiler_params=pltpu.CompilerParams(
            dimension_semantics=("arbitrary", "arbitrary"), vmem_limit_bytes=VMEM_LIMIT),
        name="inproj_rglru",
    )(x, *consts)


def _select_bias(gate, i):
    nb = gate.shape[0]
    rowi = lax.broadcasted_iota(jnp.int32, gate.shape, 0)
    rowf = rowi.astype(F32)
    ninf = -jnp.inf
    g = jnp.where(rowi < i, gate, ninf)
    sel = rowi == i
    for _ in range(min(MOBA_TOPK, i)):
        mx = jnp.max(g, axis=0, keepdims=True)
        cand = jnp.logical_and(g == mx, g > ninf)
        first = jnp.min(jnp.where(cand, rowf, float(nb)), axis=0, keepdims=True)
        pick = jnp.logical_and(rowf == first, cand)
        sel = jnp.logical_or(sel, pick)
        g = jnp.where(pick, ninf, g)
    return jnp.where(sel, 0.0, NEG)


def _attn_kernel(q_ref, k_ref, v_ref, kx_ref, slope_ref, zero_ref, o_ref,
                 kp_ref, vt_ref, km_ref, qp_ref, s_ref, p_ref):
    L = MOBA_BLOCK
    nb = k_ref.shape[1] // L
    W = HEAD_PAIR * HEAD_DIM
    rows8 = (L // V7X_SUBLANES, V7X_SUBLANES, L)

    lane1 = lax.broadcasted_iota(jnp.int32, (1, W), 1)
    for j in range(nb):
        kj = k_ref[0, j * L:(j + 1) * L, :]
        kp_ref[j, :, 0:W] = kj.astype(BF16)
        kp_ref[j, :, W:2 * W] = kx_ref[j * L:(j + 1) * L, :]
        kmean = jnp.mean(kj, axis=0, keepdims=True)
        for hh in range(HEAD_PAIR):
            in_head = jnp.logical_and(lane1 >= hh * HEAD_DIM, lane1 < (hh + 1) * HEAD_DIM)
            km_ref[hh * nb + j:hh * nb + j + 1, :] = jnp.where(in_head, kmean, 0.0)
        vt = v_ref[0, j * L:(j + 1) * L, :].astype(F32).T.astype(BF16)
        for hh in range(HEAD_PAIR):
            vt_ref[hh, 0:HEAD_DIM, j * L:(j + 1) * L] = vt[hh * HEAD_DIM:(hh + 1) * HEAD_DIM, :]
    for hh in range(HEAD_PAIR):
        vt_ref[hh, HEAD_DIM:HEAD_DIM + ONES_ROWS, :] = jnp.ones((ONES_ROWS, nb * L), BF16)

    causal = lax.broadcasted_iota(jnp.int32, (L, L), 0) <= lax.broadcasted_iota(jnp.int32, (L, L), 1)
    pad_rows = jnp.zeros((W - nb - N_ALIBI_ROWS, L), F32)
    qscale = LOG2E / np.sqrt(HEAD_DIM)

    dyn0 = zero_ref[0]

    shared = {}
    qrow = lax.broadcasted_iota(jnp.int32, (W, L), 0)

    class Stream:
        pass

    def prepare(n, i, hh):
        st = Stream()
        st.i, st.hh, st.slot, st.qslot = i, hh, n % 2, n % qp_ref.shape[0]
        if hh == 0:
            qf = q_ref[0, i * L:(i + 1) * L, :]
            qt_raw = qf.T
            shared["gate"] = _dot_split(km_ref[...], qt_raw)
            shared["qt"] = qt_raw * qscale
        gate = shared["gate"][hh * nb:(hh + 1) * nb, :]
        qt = jnp.where(jnp.logical_and(qrow >= hh * HEAD_DIM, qrow < (hh + 1) * HEAD_DIM), shared["qt"], 0.0)
        qp_ref[st.qslot] = jnp.concatenate([qt, _select_bias(gate, i), slope_ref[hh], pad_rows],
                                           axis=0).astype(BF16)
        st.m8 = None
        return st

    def score_step(st, c):
        s = _dot(kp_ref[c], qp_ref[st.qslot])
        if c == st.i:
            s = jnp.where(causal, s, NEG)
        s_ref[st.slot + dyn0, c] = s
        m8 = jnp.max(s.reshape(rows8), axis=0)
        st.m8 = m8 if st.m8 is None else jnp.maximum(st.m8, m8)

    def prob_step(st, c):
        p_ref[st.slot, c * L:(c + 1) * L, :] = jnp.exp2(s_ref[st.slot + dyn0, c] - st.m).astype(BF16)

    def finish(st):
        k_len = (st.i + 1) * L
        acc = _dot(vt_ref[st.hh, :, 0:k_len], p_ref[st.slot, 0:k_len, :])
        return acc[0:HEAD_DIM, :] / acc[HEAD_DIM:HEAD_DIM + 1, :]

    def order(i):
        return [i] + list(range(i))

    streams = [(i, hh) for i in range(nb) for hh in range(HEAD_PAIR)]
    outs = []
    lookahead = qp_ref.shape[0] - 1
    ready = [prepare(n, *streams[n]) for n in range(min(lookahead, len(streams)))]

    def emit_pv(st):
        outs.append(finish(st))
        if st.hh == HEAD_PAIR - 1:
            o_ref[0, st.i * L:(st.i + 1) * L, :] = jnp.concatenate(outs, axis=0).T.astype(o_ref.dtype)
            outs.clear()

    for n in range(len(streams) + 2):
        if n + lookahead < len(streams):
            ready.append(prepare(n + lookahead, *streams[n + lookahead]))
        st = ready[n] if n < len(streams) else None
        pr = ready[n - 1] if 0 <= n - 1 < len(streams) else None
        pv = ready[n - 2] if 0 <= n - 2 < len(streams) else None
        cur_steps = order(st.i) if st is not None else []
        prev_steps = order(pr.i) if pr is not None else []
        n_steps = max(len(cur_steps), len(prev_steps))
        for k in range(n_steps):
            if pv is not None and k == n_steps // 2:
                emit_pv(pv)
                pv = None
            if k < len(cur_steps):
                score_step(st, cur_steps[k])
            if k < len(prev_steps):
                prob_step(pr, prev_steps[k])
        if pv is not None:
            emit_pv(pv)
        if st is not None:
            st.m = jnp.max(st.m8, axis=0, keepdims=True)


def _attn_call(q, k, v, kx, slope_rows):
    B, T, _ = q.shape
    L = MOBA_BLOCK
    nb = T // L
    W = HEAD_PAIR * HEAD_DIM
    npairs = N_HEADS // HEAD_PAIR
    seq_spec = pl.BlockSpec((1, T, W), lambda b, g: (b, 0, g))
    return pl.pallas_call(
        _attn_kernel,
        grid=(B, npairs),
        in_specs=[seq_spec, seq_spec, seq_spec, _const_spec(kx.shape),
                  pl.BlockSpec((HEAD_PAIR, N_ALIBI_ROWS, L), lambda b, g: (g, 0, 0)),
                  pl.BlockSpec(memory_space=pltpu.SMEM)],
        out_specs=seq_spec,
        out_shape=jax.ShapeDtypeStruct((B, T, ATT_WIDTH), BF16),
        scratch_shapes=[pltpu.VMEM((nb, L, 2 * W), BF16),
                        pltpu.VMEM((HEAD_PAIR, HEAD_DIM + ONES_ROWS, T), BF16),
                        pltpu.VMEM((HEAD_PAIR * nb, W), F32),
                        pltpu.VMEM((ATTN_LOOKAHEAD + 1, 2 * W, L), BF16),
                        pltpu.VMEM((2, nb, L, L), F32),
                        pltpu.VMEM((2, T, L), BF16)],
        compiler_params=pltpu.CompilerParams(
            dimension_semantics=("arbitrary", "arbitrary"), vmem_limit_bytes=VMEM_LIMIT),
        name="moba_attention",
    )(q, k, v, kx, slope_rows, jnp.zeros((1,), jnp.int32))


def _ffn_kernel(x_ref, ylru_ref, yatt_ref, p_ref, wout_ref, lnf_ref, wup_ref, fw_ref, fb_ref, wdown_ref,
                lnp_ref, wg_ref, wp_ref, o_ref, gbuf, gcar):
    t = pl.program_id(1)
    tt = x_ref.shape[1]
    d_ff = wdown_ref.shape[0]
    pad = V7X_SUBLANES

    @pl.when(t == 0)
    def _():
        gcar[...] = jnp.zeros_like(gcar)

    def chain(r):
        rows = slice(r * FFN_SUB_ROWS, (r + 1) * FFN_SUB_ROWS)
        sub = FFN_SUB_ROWS
        h = (x_ref[0, rows, :] + _dot(ylru_ref[0, rows, :], wout_ref[0:LRU_WIDTH, :])
             + _dot(yatt_ref[0, rows, :], wout_ref[LRU_WIDTH:LRU_WIDTH + ATT_WIDTH, :]))
        yield
        u = _rmsnorm(h, lnf_ref[...]).astype(BF16)
        yield
        acc = jnp.zeros_like(h)
        for c in range(d_ff // FF_CHUNK):
            lo = c * FF_CHUNK
            g = _dot(u, wup_ref[:, lo:lo + FF_CHUNK])
            vv = _dot(u, wup_ref[:, d_ff + lo:d_ff + lo + FF_CHUNK])
            gbuf[r, 0:pad, :] = gcar[c]
            gbuf[r, pad:pad + sub, :] = g
            gcar[c] = g[sub - pad:sub, :]
            gc = fb_ref[:, lo:lo + FF_CHUNK] + fw_ref[FFN_CONV - 1:FFN_CONV, lo:lo + FF_CHUNK] * g
            for kk in range(FFN_CONV - 1):
                gc = gc + fw_ref[kk:kk + 1, lo:lo + FF_CHUNK] * gbuf[r, pl.ds(pad - (FFN_CONV - 1) + kk, sub), :]
            act = (jax.nn.gelu(gc) * vv).astype(BF16)
            acc = acc + _dot(act, wdown_ref[lo:lo + FF_CHUNK, :])
            yield
        h = h + acc
        un = _rmsnorm(h, lnp_ref[...]).astype(BF16)
        yield
        gate = jax.nn.sigmoid(_dot(un, wg_ref[...]))
        o_ref[0, rows, :] = h + gate * _dot(p_ref[0, rows, :].astype(BF16), wp_ref[...])
        yield

    _run_skewed([chain(r) for r in range(tt // FFN_SUB_ROWS)], n_phase=4 + d_ff // FF_CHUNK)


def _ffn_call(x, ylru, yatt, p, wout, lnf, wup, fw, fb, wdown, lnp, wg, wp):
    B, T, D = x.shape
    tt = min(ROW_TILE, T)
    d_ff = wdown.shape[0]
    row_spec = lambda w: pl.BlockSpec((1, tt, w), lambda b, t: (b, t, 0))
    consts = (wout, lnf, wup, fw, fb, wdown, lnp, wg, wp)
    return pl.pallas_call(
        _ffn_kernel,
        grid=(B, T // tt),
        in_specs=[row_spec(D), row_spec(LRU_WIDTH), row_spec(ATT_WIDTH), row_spec(p.shape[-1])]
                 + [_const_spec(c.shape) for c in consts],
        out_specs=row_spec(D),
        out_shape=jax.ShapeDtypeStruct((B, T, D), x.dtype),
        scratch_shapes=[pltpu.VMEM((tt // FFN_SUB_ROWS, FFN_SUB_ROWS + V7X_SUBLANES, FF_CHUNK), F32),
                        pltpu.VMEM((d_ff // FF_CHUNK, V7X_SUBLANES, FF_CHUNK), F32)],
        compiler_params=pltpu.CompilerParams(
            dimension_semantics=("arbitrary", "arbitrary"), vmem_limit_bytes=VMEM_LIMIT),
        name="outproj_ffn_ple",
    )(x, ylru, yatt, p, *consts)


def _block_diag(w):
    n, c, d = w.shape
    return jnp.einsum("ncd,nm->ncmd", w, jnp.eye(n, dtype=w.dtype)).reshape(n * c, n * d)


def _gate_weights(wa, wx):
    per = V7X_MXU_DIM // LRU_BLOCK_DIM
    groups = []
    for hf in range(LRU_WIDTH // V7X_MXU_DIM):
        sl = slice(hf * per, (hf + 1) * per)
        groups.append(jnp.concatenate([_block_diag(wa[sl]), _block_diag(wx[sl])], axis=1))
    return jnp.stack(groups).astype(BF16)


def _split3_bf16(s):
    s1 = s.astype(BF16).astype(F32)
    s2 = (s - s1).astype(BF16).astype(F32)
    s3 = (s - s1 - s2).astype(BF16).astype(F32)
    return s1, s2, s3


def _attention_constants(T):
    nb = T // MOBA_BLOCK
    W = HEAD_PAIR * HEAD_DIM
    pos = np.arange(T, dtype=np.int32)
    onehot = (pos[:, None] // MOBA_BLOCK == np.arange(nb)[None, :]).astype(np.float32)
    hi = ((pos >> 6) << 6).astype(np.float32)[:, None]
    lo = (pos & 63).astype(np.float32)[:, None]
    cols = [onehot, hi, hi, hi, lo, lo, lo]
    kx = np.concatenate(cols + [np.zeros((T, W - nb - 6), np.float32)], axis=1)
    slopes = np.float32(2.0) ** (np.float32(-8.0) * np.arange(1, N_HEADS + 1, dtype=np.float32) / np.float32(N_HEADS))
    s1, s2, s3 = _split3_bf16(jnp.asarray(slopes * np.float32(LOG2E)))
    rows = jnp.stack([s1, s2, s3, s1, s2, s3] + [jnp.zeros_like(s1)] * (N_ALIBI_ROWS - 6), axis=1)
    slope_rows = jnp.broadcast_to(rows[:, :, None], (N_HEADS, N_ALIBI_ROWS, MOBA_BLOCK)).astype(F32)
    return jnp.asarray(kx, dtype=BF16), slope_rows


def kernel(x, p, ln_mix, w_in, conv_w, conv_b, gate_a_w, gate_a_b, gate_x_w, gate_x_b, lru_lambda, q_gain,
           k_gain, w_out, ln_ffn, w_up, ffn_conv_w, ffn_conv_b, w_down, ln_ple, w_ple_gate, w_ple_proj):
    B, T, D = x.shape
    depth = w_in.shape[0]
    assert T % MOBA_BLOCK == 0 and T % min(ROW_TILE, T) == 0
    row = lambda a: a.reshape(1, -1)
    seg = jnp.asarray(np.kron(np.eye(V7X_MXU_DIM // HEAD_DIM), np.ones((HEAD_DIM, HEAD_DIM))), dtype=BF16)
    kx, slope_rows = _attention_constants(T)
    h = x
    for i in range(depth):
        ylru, q, k, v = _inproj_call(
            h, row(ln_mix[i]), w_in[i].astype(BF16), conv_w[i], row(conv_b[i]),
            _gate_weights(gate_a_w[i], gate_x_w[i]), row(gate_a_b[i]), row(gate_x_b[i]), row(lru_lambda[i]),
            row(jnp.tile(q_gain[i], N_HEADS)), row(jnp.tile(k_gain[i], N_HEADS)), seg)
        yatt = _attn_call(q, k, v, kx, slope_rows)
        h = _ffn_call(
            h, ylru, yatt, p[i].astype(h.dtype), w_out[i].astype(BF16), row(ln_ffn[i]), w_up[i].astype(BF16),
            ffn_conv_w[i], row(ffn_conv_b[i]), w_down[i].astype(BF16), row(ln_ple[i]),
            w_ple_gate[i].astype(BF16), w_ple_proj[i].astype(BF16))
    return h
```
